```python
import math
import jax, jax.numpy as jnp
from jax import lax
import numpy as np

D_MODEL = 1024
BATCH = 2
SEQ = 8192
DEPTH = 2
DEC_BATCH = 32
DEC_SEQ = 4
PAST_LEN = 16384
PAGE_SIZE = 128

N_MEM = 256
CHUNK = 128
SGU_GROUPS = 4
SGU_WIDTH = 512
SGU_GROUP_DIM = SGU_WIDTH // SGU_GROUPS
DA_HEADS = 8
DA_QK_DIM = 64
DA_V_DIM = 2 * DA_QK_DIM
DA_WIDTH = DA_HEADS * DA_V_DIM
MEM_HEADS = 4
MEM_HEAD_DIM = 128
MEM_WIDTH = MEM_HEADS * MEM_HEAD_DIM
N_IN = 2 * SGU_WIDTH + 3 * DA_WIDTH + MEM_WIDTH + 3 * D_MODEL
D_FF = 2816
N_EXPERTS = 8
TOP_K = 2
D_FF_EXPERT = 3584
MOE_BLOCK = 128
N_DENSE = (DEPTH + 1) // 2
N_MOE = DEPTH // 2
Q_BLOCK = 128
EPS = 1e-5

kernel_name = "hybrid_gmlp_diffattn_decoder_step"


def rmsnorm(x, g):
    xf = x.astype(jnp.float32)
    y = xf * lax.rsqrt(jnp.mean(xf * xf, axis=-1, keepdims=True) + EPS)
    return (y * g.astype(jnp.float32)).astype(x.dtype)


def layernorm(x, g, b):
    xf = x.astype(jnp.float32)
    mu = jnp.mean(xf, axis=-1, keepdims=True)
    xc = xf - mu
    var = jnp.mean(xc * xc, axis=-1, keepdims=True)
    return (xc * lax.rsqrt(var + EPS) * g.astype(jnp.float32) + b.astype(jnp.float32)).astype(x.dtype)


def alibi_slopes(n):
    return jnp.asarray([2.0 ** (-8.0 * (i + 1) / n) for i in range(n)], jnp.float32)


def split_proj(z):
    cuts = np.cumsum([SGU_WIDTH, SGU_WIDTH, DA_WIDTH, DA_WIDTH, DA_WIDTH, MEM_WIDTH]).tolist()
    return jnp.split(z, cuts, axis=-1)


def sgu_branch(u_raw, v_raw, ln_g, ln_b, w_s, b_s, chunk_len):
    B, T, _ = u_raw.shape
    u = jax.nn.gelu(u_raw, approximate=False)
    v = layernorm(jax.nn.gelu(v_raw, approximate=False), ln_g, ln_b)
    vc = v.reshape(B, T // chunk_len, chunk_len, SGU_GROUPS, SGU_GROUP_DIM)
    mask = jnp.tril(jnp.ones((chunk_len, chunk_len), jnp.bool_)).astype(w_s.dtype)
    w = w_s[:, :chunk_len, :chunk_len] * mask
    mixed = jnp.einsum("gts,bcsgd->bctgd", w, vc) + b_s[:, :chunk_len].T[None, None, :, :, None]
    return u * mixed.reshape(B, T, SGU_WIDTH), v


def diff_lambda(lq1, lk1, lq2, lk2, lam_init):
    f = jnp.float32
    return jnp.exp(jnp.sum(lq1.astype(f) * lk1.astype(f))) - jnp.exp(jnp.sum(lq2.astype(f) * lk2.astype(f))) + lam_init


def diff_attend(q, k, v, q_pos, k_pos, lam, slopes):
    dist = (q_pos[:, None] - k_pos[None, :]).astype(jnp.float32)
    bias = -slopes[:, None, None] * dist
    causal = dist >= 0
    scale = DA_QK_DIM ** -0.5

    def softmax_map(qa, ka):
        s = jnp.einsum("bqhd,bkhd->bhqk", qa, ka, preferred_element_type=jnp.float32) * scale + bias
        return jax.nn.softmax(jnp.where(causal, s, -jnp.inf), axis=-1)

    p = softmax_map(q[..., :DA_QK_DIM], k[..., :DA_QK_DIM]) - lam * softmax_map(q[..., DA_QK_DIM:], k[..., DA_QK_DIM:])
    return jnp.einsum("bhqk,bkhd->bqhd", p.astype(v.dtype), v, preferred_element_type=jnp.float32)


def diff_out(o, subln_g, lam_init, dtype):
    B, T = o.shape[:2]
    return (rmsnorm(o, subln_g) * (1.0 - lam_init)).astype(dtype).reshape(B, T, DA_WIDTH)


def mem_kv(mem, g, wk, wv):
    B = mem.shape[0]
    m = rmsnorm(mem, g)
    return ((m @ wk).reshape(B, N_MEM, MEM_HEADS, MEM_HEAD_DIM),
            (m @ wv).reshape(B, N_MEM, MEM_HEADS, MEM_HEAD_DIM))


def mem_attend(qm, mk, mv):
    B, T = qm.shape[:2]
    q = qm.reshape(B, T, MEM_HEADS, MEM_HEAD_DIM)
    s = jnp.einsum("bqhd,bkhd->bhqk", q, mk, preferred_element_type=jnp.float32) * (MEM_HEAD_DIM ** -0.5)
    p = jax.nn.softmax(s, axis=-1)
    o = jnp.einsum("bhqk,bkhd->bqhd", p.astype(mv.dtype), mv)
    return o.reshape(B, T, MEM_WIDTH)


def merge_branches(a, b, m, g_raw, lp):
    g = jax.nn.sigmoid((g_raw + lp["b_gate"]).astype(jnp.float32)).astype(a.dtype)
    g_a, g_b, g_m = jnp.split(g, 3, axis=-1)
    y = g_a * (a @ lp["w_br_a"]) + g_b * (b @ lp["w_br_b"]) + g_m * (m @ lp["w_br_m"])
    return y @ lp["w_out"]


def mixer_prompt(h, mk, mv, lp, lam, lam_init, slopes):
    B, T, _ = h.shape
    u, v, q, k, vv, qm, g_raw = split_proj(h @ lp["w_in"])
    a, _ = sgu_branch(u, v, lp["sgu_ln_g"], lp["sgu_ln_b"], lp["sgu_w"], lp["sgu_b"], CHUNK)
    q = q.reshape(B, T, DA_HEADS, DA_V_DIM)
    k = k.reshape(B, T, DA_HEADS, DA_V_DIM)
    vv = vv.reshape(B, T, DA_HEADS, DA_V_DIM)
    k_pos = jnp.arange(T, dtype=jnp.int32)

    def q_block(i):
        start = i * Q_BLOCK
        qb = lax.dynamic_slice_in_dim(q, start, Q_BLOCK, axis=1)
        return diff_attend(qb, k, vv, start + jnp.arange(Q_BLOCK, dtype=jnp.int32), k_pos, lam, slopes)

    o = lax.map(q_block, jnp.arange(T // Q_BLOCK, dtype=jnp.int32))
    o = jnp.moveaxis(o, 0, 1).reshape(B, T, DA_HEADS, DA_V_DIM)
    b_out = diff_out(o, lp["subln_g"], lam_init, h.dtype)
    m_out = mem_attend(qm, mk, mv)
    return merge_branches(a, b_out, m_out, g_raw, lp), k, vv


def mixer_sample(h, ck, cv, mk, mv, page_table, lp, lam, lam_init, slopes):
    B, T, _ = h.shape
    u, v, q, k, vv, qm, g_raw = split_proj(h @ lp["w_in"])
    a, v_rows = sgu_branch(u, v, lp["sgu_ln_g"], lp["sgu_ln_b"], lp["sgu_w"], lp["sgu_b"], min(T, CHUNK))
    q = q.reshape(B, T, DA_HEADS, DA_V_DIM)
    k = k.reshape(B, T, DA_HEADS, DA_V_DIM)
    vv = vv.reshape(B, T, DA_HEADS, DA_V_DIM)
    q_pos = PAST_LEN + jnp.arange(T, dtype=jnp.int32)
    k_pos = jnp.arange(PAST_LEN + T, dtype=jnp.int32)

    def one_seq(args):
        pages, qs, ks, vs = args
        n_rows = pages.shape[0] * PAGE_SIZE
        kp = ck[pages].reshape(n_rows, DA_HEADS, DA_V_DIM)
        vp = cv[pages].reshape(n_rows, DA_HEADS, DA_V_DIM)
        k_all = jnp.concatenate([kp, ks.astype(kp.dtype)], axis=0)[None]
        v_all = jnp.concatenate([vp, vs.astype(vp.dtype)], axis=0)[None]
        return diff_attend(qs[None], k_all, v_all, q_pos, k_pos, lam, slopes)[0]

    o = lax.map(one_seq, (page_table, q, k, vv))
    b_out = diff_out(o, lp["subln_g"], lam_init, h.dtype)
    m_out = mem_attend(qm, mk, mv)
    return merge_branches(a, b_out, m_out, g_raw, lp), k, vv, v_rows


def swiglu(x, wg, wu, wd):
    return (jax.nn.silu(x @ wg) * (x @ wu)) @ wd


def moe_ffn(h, w_router, b_router, wg, wu, wd):
    N, D = h.shape
    logits = (h @ w_router).astype(jnp.float32) + b_router.astype(jnp.float32)
    top_vals, top_idx = lax.top_k(logits, TOP_K)
    gates = jax.nn.softmax(top_vals, axis=-1)
    NK = N * TOP_K
    flat_e = top_idx.reshape(NK).astype(jnp.int32)
    flat_tok = jnp.repeat(jnp.arange(N, dtype=jnp.int32), TOP_K)
    flat_g = gates.reshape(NK)
    order = jnp.argsort(flat_e)
    e_sorted = flat_e[order]
    tok_sorted = flat_tok[order]
    g_sorted = flat_g[order]
    counts = jnp.bincount(flat_e, length=N_EXPERTS).astype(jnp.int32)
    padded = (counts + MOE_BLOCK - 1) // MOE_BLOCK * MOE_BLOCK
    pad_end = jnp.cumsum(padded)
    pad_start = pad_end - padded
    start = jnp.cumsum(counts) - counts
    dest = pad_start[e_sorted] + (jnp.arange(NK, dtype=jnp.int32) - start[e_sorted])
    L = (NK + MOE_BLOCK - 1) // MOE_BLOCK * MOE_BLOCK + N_EXPERTS * MOE_BLOCK
    n_blocks = L // MOE_BLOCK
    tok_buf = jnp.zeros((L,), jnp.int32).at[dest].set(tok_sorted)
    block_expert = jnp.clip(jnp.searchsorted(pad_end, jnp.arange(n_blocks, dtype=jnp.int32) * MOE_BLOCK, side="right"),
                            0, N_EXPERTS - 1)
    x_buf = h[tok_buf].reshape(n_blocks, MOE_BLOCK, D)

    def run_block(args):
        xb, e = args
        return swiglu(xb, wg[e], wu[e], wd[e])

    y_buf = lax.map(run_block, (x_buf, block_expert)).reshape(L, D)
    return jnp.zeros_like(h).at[tok_sorted].add(y_buf[dest] * g_sorted[:, None].astype(h.dtype))


def setup_inputs(seed: int = 0) -> dict:
    key = jax.random.key(seed)
    ks = list(jax.random.split(key, 48))
    it = iter(ks)

    def nrm(shape, scale):
        return jax.random.normal(next(it), shape, jnp.float32) * scale

    n_pages = PAST_LEN // PAGE_SIZE
    n_used = DEC_BATCH * n_pages
    n_phys = n_used + n_used // 4
    D = D_MODEL
    inp = {}
    inp["x_prompt"] = nrm((BATCH, SEQ, D), 1.0)
    inp["x_sample"] = nrm((DEC_BATCH, DEC_SEQ, D), 1.0)
    inp["mem_prompt"] = nrm((BATCH, N_MEM, D), 1.0)
    inp["cache_attn_k"] = nrm((DEPTH, n_phys, PAGE_SIZE, DA_HEADS, DA_V_DIM), 1.0)
    inp["cache_attn_v"] = nrm((DEPTH, n_phys, PAGE_SIZE, DA_HEADS, DA_V_DIM), 1.0)
    inp["cache_mem_k"] = nrm((DEPTH, DEC_BATCH, N_MEM, MEM_HEADS, MEM_HEAD_DIM), 1.0)
    inp["cache_mem_v"] = nrm((DEPTH, DEC_BATCH, N_MEM, MEM_HEADS, MEM_HEAD_DIM), 1.0)
    inp["page_table"] = jax.random.permutation(next(it), n_phys)[:n_used].reshape(DEC_BATCH, n_pages).astype(jnp.int32)
    inp["norm_mix"] = 1.0 + nrm((DEPTH, D), 0.02)
    inp["w_in"] = nrm((DEPTH, D, N_IN), D ** -0.5)
    inp["b_gate"] = nrm((DEPTH, 3 * D), 0.02)
    inp["sgu_ln_g"] = 1.0 + nrm((DEPTH, SGU_WIDTH), 0.02)
    inp["sgu_ln_b"] = nrm((DEPTH, SGU_WIDTH), 0.02)
    inp["sgu_w"] = nrm((DEPTH, SGU_GROUPS, CHUNK, CHUNK), CHUNK ** -0.5)
    inp["sgu_b"] = 1.0 + nrm((DEPTH, SGU_GROUPS, CHUNK), 0.1)
    inp["lam_q1"] = nrm((DEPTH, DA_QK_DIM), 0.1)
    inp["lam_k1"] = nrm((DEPTH, DA_QK_DIM), 0.1)
    inp["lam_q2"] = nrm((DEPTH, DA_QK_DIM), 0.1)
    inp["lam_k2"] = nrm((DEPTH, DA_QK_DIM), 0.1)
    inp["subln_g"] = 1.0 + nrm((DEPTH, DA_V_DIM), 0.02)
    inp["mem_norm"] = 1.0 + nrm((DEPTH, D), 0.02)
    inp["w_mem_k"] = nrm((DEPTH, D, MEM_WIDTH), D ** -0.5)
    inp["w_mem_v"] = nrm((DEPTH, D, MEM_WIDTH), D ** -0.5)
    inp["w_br_a"] = nrm((DEPTH, SGU_WIDTH, D), SGU_WIDTH ** -0.5)
    inp["w_br_b"] = nrm((DEPTH, DA_WIDTH, D), DA_WIDTH ** -0.5)
    inp["w_br_m"] = nrm((DEPTH, MEM_WIDTH, D), MEM_WIDTH ** -0.5)
    inp["w_out"] = nrm((DEPTH, D, D), D ** -0.5)
    inp["norm_ffn"] = 1.0 + nrm((DEPTH, D), 0.02)
    inp["w_d_gate"] = nrm((N_DENSE, D, D_FF), D ** -0.5)
    inp["w_d_up"] = nrm((N_DENSE, D, D_FF), D ** -0.5)
    inp["w_d_down"] = nrm((N_DENSE, D_FF, D), D_FF ** -0.5)
    inp["w_router"] = nrm((N_MOE, D, N_EXPERTS), D ** -0.5)
    inp["b_router"] = nrm((N_MOE, N_EXPERTS), 0.01)
    inp["w_e_gate"] = nrm((N_MOE, N_EXPERTS, D, D_FF_EXPERT), D ** -0.5)
    inp["w_e_up"] = nrm((N_MOE, N_EXPERTS, D, D_FF_EXPERT), D ** -0.5)
    inp["w_e_down"] = nrm((N_MOE, N_EXPERTS, D_FF_EXPERT, D), D_FF_EXPERT ** -0.5)
    inp["norm_final"] = 1.0 + nrm((D,), 0.02)
    return inp


def reference(x_prompt, x_sample, mem_prompt, cache_attn_k, cache_attn_v, cache_mem_k, cache_mem_v, page_table,
              norm_mix, w_in, b_gate, sgu_ln_g, sgu_ln_b, sgu_w, sgu_b, lam_q1, lam_k1, lam_q2, lam_k2, subln_g,
              mem_norm, w_mem_k, w_mem_v, w_br_a, w_br_b, w_br_m, w_out, norm_ffn,
              w_d_gate, w_d_up, w_d_down, w_router, b_router, w_e_gate, w_e_up, w_e_down, norm_final):
    slopes = alibi_slopes(DA_HEADS)
    xp, xs = x_prompt, x_sample
    p_k, p_v, p_mk, p_mv, s_k, s_v, s_cv = [], [], [], [], [], [], []
    for l in range(DEPTH):
        lp = {"w_in": w_in[l], "b_gate": b_gate[l], "sgu_ln_g": sgu_ln_g[l], "sgu_ln_b": sgu_ln_b[l],
              "sgu_w": sgu_w[l], "sgu_b": sgu_b[l], "subln_g": subln_g[l], "w_br_a": w_br_a[l],
              "w_br_b": w_br_b[l], "w_br_m": w_br_m[l], "w_out": w_out[l]}
        lam_init = 0.8 - 0.6 * math.exp(-0.3 * l)
        lam = diff_lambda(lam_q1[l], lam_k1[l], lam_q2[l], lam_k2[l], lam_init)
        mk_p, mv_p = mem_kv(mem_prompt, mem_norm[l], w_mem_k[l], w_mem_v[l])
        yp, kp, vp = mixer_prompt(rmsnorm(xp, norm_mix[l]), mk_p, mv_p, lp, lam, lam_init, slopes)
        ys, ksn, vsn, cvn = mixer_sample(rmsnorm(xs, norm_mix[l]), cache_attn_k[l], cache_attn_v[l],
                                         cache_mem_k[l], cache_mem_v[l], page_table, lp, lam, lam_init, slopes)
        xp = xp + yp
        xs = xs + ys
        hp = rmsnorm(xp, norm_ffn[l])
        hs = rmsnorm(xs, norm_ffn[l])
        if l % 2 == 0:
            d = l // 2
            xp = xp + swiglu(hp, w_d_gate[d], w_d_up[d], w_d_down[d])
            xs = xs + swiglu(hs, w_d_gate[d], w_d_up[d], w_d_down[d])
        else:
            e = l // 2
            xp = xp + moe_ffn(hp.reshape(-1, D_MODEL), w_router[e], b_router[e], w_e_gate[e], w_e_up[e],
                              w_e_down[e]).reshape(xp.shape)
            xs = xs + moe_ffn(hs.reshape(-1, D_MODEL), w_router[e], b_router[e], w_e_gate[e], w_e_up[e],
                              w_e_down[e]).reshape(xs.shape)
        p_k.append(kp)
        p_v.append(vp)
        p_mk.append(mk_p)
        p_mv.append(mv_p)
        s_k.append(ksn)
        s_v.append(vsn)
        s_cv.append(cvn)
    y_prompt = rmsnorm(xp, norm_final)
    y_sample = rmsnorm(xs, norm_final)
    prompt_attn_k = jnp.stack(p_k)
    prompt_attn_v = jnp.stack(p_v)
    prompt_mem_k = jnp.stack(p_mk)
    prompt_mem_v = jnp.stack(p_mv)
    sample_attn_k = jnp.stack(s_k)
    sample_attn_v = jnp.stack(s_v)
    sample_chunk_v = jnp.stack(s_cv)
    return (y_prompt, y_sample, prompt_attn_k, prompt_attn_v, prompt_mem_k, prompt_mem_v,
            sample_attn_k, sample_attn_v, sample_chunk_v)
```

```python
import functools
import math

import numpy as np
import jax
import jax.numpy as jnp
from jax import lax
from jax.experimental import pallas as pl
from jax.experimental.pallas import tpu as pltpu

F32 = jnp.float32
BF16 = jnp.bfloat16

D_MODEL = 1024
CHUNK = 128
SGU_GROUPS = 4
SGU_WIDTH = 512
DA_HEADS = 8
DA_QK_DIM = 64
DA_V_DIM = 128
DA_WIDTH = DA_HEADS * DA_V_DIM
MEM_HEADS = 4
MEM_HEAD_DIM = 128
MEM_WIDTH = MEM_HEADS * MEM_HEAD_DIM
N_MEM = 256
N_EXPERTS = 8
TOP_K = 2
PAGE_SIZE = 128
EPS = 1e-5
NEG_INF = float("-inf")

VMEM_LIMIT_BYTES = 48 * 1024 * 1024
LANES = 128
ROUTER_LANES = 128
QM_PAD_ROWS = 16


def _cparams(n_axes):
    return pltpu.CompilerParams(
        dimension_semantics=("arbitrary",) * n_axes, vmem_limit_bytes=VMEM_LIMIT_BYTES)


def _rms(x, g):
    ms = jnp.mean(x * x, axis=-1, keepdims=True)
    return x * lax.rsqrt(ms + EPS) * g


def _gelu(x):
    return 0.5 * x * (1.0 + lax.erf(x * (1.0 / math.sqrt(2.0))))


def _dot(a, b):
    return jnp.dot(a, b, preferred_element_type=F32)


def _dot_nt(a, b):
    return lax.dot_general(a, b, (((1,), (1,)), ((), ())), preferred_element_type=F32)


def _norm_matmul_kernel(x_ref, g_ref, w_ref, b_ref, o_ref, h_scr, *, sigmoid):
    @pl.when(pl.program_id(1) == 0)
    def _():
        h_scr[...] = _rms(x_ref[...], g_ref[...]).astype(BF16)

    acc = _dot(h_scr[...], w_ref[...])
    if sigmoid:
        acc = 1.0 / (1.0 + jnp.exp(-(acc + b_ref[...])))
    o_ref[...] = acc.astype(o_ref.dtype)


def norm_matmul(x, g, w, bias, *, sigmoid, out_dtype, tm, tn):
    m, d = x.shape
    n = w.shape[1]
    return pl.pallas_call(
        functools.partial(_norm_matmul_kernel, sigmoid=sigmoid),
        out_shape=jax.ShapeDtypeStruct((m, n), out_dtype),
        grid=(m // tm, n // tn),
        in_specs=[
            pl.BlockSpec((tm, d), lambda i, j: (i, 0)),
            pl.BlockSpec((1, d), lambda i, j: (0, 0)),
            pl.BlockSpec((d, tn), lambda i, j: (0, j)),
            pl.BlockSpec((1, tn), lambda i, j: (0, j)),
        ],
        out_specs=pl.BlockSpec((tm, tn), lambda i, j: (i, j)),
        scratch_shapes=[pltpu.VMEM((tm, d), BF16)],
        compiler_params=_cparams(2),
        name="norm_matmul_sig" if sigmoid else "norm_matmul",
    )(x, g.reshape(1, d), w, bias.reshape(1, n))


def _norm_kv_kernel(x_ref, g_ref, w_ref, kf_ref, vf_ref, kb_ref, vb_ref, h_scr):
    j = pl.program_id(1)

    @pl.when(j == 0)
    def _():
        h_scr[...] = _rms(x_ref[...], g_ref[...]).astype(BF16)

    acc = _dot(h_scr[...], w_ref[...])

    @pl.when(j == 0)
    def _():
        kf_ref[...] = acc
        kb_ref[...] = acc.astype(BF16)

    @pl.when(j == 1)
    def _():
        vf_ref[...] = acc
        vb_ref[...] = acc.astype(BF16)


def norm_kv(x, g, w_kv, *, tm):
    m, d = x.shape
    n = w_kv.shape[1] // 2
    blk = pl.BlockSpec((tm, n), lambda i, j: (i, 0))
    return pl.pallas_call(
        _norm_kv_kernel,
        out_shape=(jax.ShapeDtypeStruct((m, n), F32), jax.ShapeDtypeStruct((m, n), F32),
                   jax.ShapeDtypeStruct((m, n), BF16), jax.ShapeDtypeStruct((m, n), BF16)),
        grid=(m // tm, 2),
        in_specs=[
            pl.BlockSpec((tm, d), lambda i, j: (i, 0)),
            pl.BlockSpec((1, d), lambda i, j: (0, 0)),
            pl.BlockSpec((d, n), lambda i, j: (0, j)),
        ],
        out_specs=(blk, blk, blk, blk),
        scratch_shapes=[pltpu.VMEM((tm, d), BF16)],
        compiler_params=_cparams(2),
        name="norm_kv",
    )(x, g.reshape(1, d), w_kv)


def _sgu_kernel(u_ref, v_ref, lng_ref, lnb_ref, w_ref, bias_ref, a_ref, vn_ref, *, n_chunks):
    u = _gelu(u_ref[...].astype(F32))
    v = _gelu(v_ref[...].astype(F32))
    mu = jnp.mean(v, axis=-1, keepdims=True)
    vc = v - mu
    var = jnp.mean(vc * vc, axis=-1, keepdims=True)
    vn = vc * lax.rsqrt(var + EPS) * lng_ref[...] + lnb_ref[...]
    vn_ref[...] = vn
    vnb = vn.astype(BF16)
    bias = bias_ref[...]
    for c in range(n_chunks):
        rows = slice(c * CHUNK, (c + 1) * CHUNK)
        for g in range(SGU_GROUPS):
            cols = slice(g * CHUNK, (g + 1) * CHUNK)
            mixed = _dot(w_ref[g], vnb[rows, cols]) + bias[:, cols]
            a_ref[rows, cols] = (u[rows, cols] * mixed).astype(a_ref.dtype)


def sgu(z, ln_g, ln_b, w_mix, bias_full, *, n_chunks):
    m = z.shape[0]
    rows = n_chunks * CHUNK
    return pl.pallas_call(
        functools.partial(_sgu_kernel, n_chunks=n_chunks),
        out_shape=(jax.ShapeDtypeStruct((m, SGU_WIDTH), BF16), jax.ShapeDtypeStruct((m, SGU_WIDTH), F32)),
        grid=(m // rows,),
        in_specs=[
            pl.BlockSpec((rows, SGU_WIDTH), lambda i: (i, 0)),
            pl.BlockSpec((rows, SGU_WIDTH), lambda i: (i, 1)),
            pl.BlockSpec((1, SGU_WIDTH), lambda i: (0, 0)),
            pl.BlockSpec((1, SGU_WIDTH), lambda i: (0, 0)),
            pl.BlockSpec((SGU_GROUPS, CHUNK, CHUNK), lambda i: (0, 0, 0)),
            pl.BlockSpec((CHUNK, SGU_WIDTH), lambda i: (0, 0)),
        ],
        out_specs=(pl.BlockSpec((rows, SGU_WIDTH), lambda i: (i, 0)),
                   pl.BlockSpec((rows, SGU_WIDTH), lambda i: (i, 0))),
        compiler_params=_cparams(1),
        name="sgu",
    )(z, z, ln_g.reshape(1, -1), ln_b.reshape(1, -1), w_mix, bias_full)


def _diff_lambda(lamp, lam_init):
    s1 = jnp.sum(lamp[0:1, :] * lamp[1:2, :], axis=-1, keepdims=True)
    s2 = jnp.sum(lamp[2:3, :] * lamp[3:4, :], axis=-1, keepdims=True)
    return jnp.exp(s1) - jnp.exp(s2) + lam_init


def _sub_norm(o, g, lam_init):
    return _rms(o, g) * (1.0 - lam_init)


def _diff_attn_kernel(slopes_ref, lamp_ref, q_ref, k_ref, v_ref, g_ref, o_ref, *, tq, lam_init):
    h = pl.program_id(1)
    qi = pl.program_id(2)
    slope = slopes_ref[h]
    scale = DA_QK_DIM ** -0.5

    q = q_ref[...] * jnp.asarray(scale, BF16)
    lane = lax.broadcasted_iota(jnp.int32, q.shape, 1)
    zero = jnp.zeros_like(q)
    qz = (jnp.where(lane < DA_QK_DIM, q, zero), jnp.where(lane >= DA_QK_DIM, q, zero))

    row = lax.broadcasted_iota(jnp.int32, (tq, tq), 0)
    col = lax.broadcasted_iota(jnp.int32, (tq, tq), 1)
    rel = (col - row).astype(F32) * slope
    rel_diag = jnp.where(col <= row, rel, NEG_INF)

    def update(carry, kt, vt, bias, shift):
        new = []
        for (m, l, acc), qm in zip(carry, qz):
            t = _dot_nt(qm, kt) + bias
            m_new = jnp.maximum(m, jnp.max(t, axis=-1, keepdims=True) + shift)
            alpha = jnp.exp(m - m_new)
            p = jnp.exp(t - (m_new - shift))
            l_new = alpha * l + jnp.sum(p, axis=-1, keepdims=True)
            acc_new = alpha * acc + _dot(p.astype(BF16), vt)
            new.append((m_new, l_new, acc_new))
        return tuple(new)

    def body(j, carry):
        start = pl.multiple_of(j * tq, tq)
        kt = k_ref[pl.ds(start, tq), :]
        vt = v_ref[pl.ds(start, tq), :]
        shift = ((j - qi) * tq).astype(F32) * slope
        return update(carry, kt, vt, rel, shift)

    one = (jnp.full((tq, 1), NEG_INF, F32), jnp.zeros((tq, 1), F32), jnp.zeros((tq, DA_V_DIM), F32))
    carry = lax.fori_loop(0, qi, body, (one, one))
    start = pl.multiple_of(qi * tq, tq)
    (_, l1, acc1), (_, l2, acc2) = update(carry, k_ref[pl.ds(start, tq), :], v_ref[pl.ds(start, tq), :], rel_diag,
                                          0.0)
    lam = _diff_lambda(lamp_ref[...], lam_init)
    o = acc1 / l1 - lam * (acc2 / l2)
    o_ref[...] = _sub_norm(o, g_ref[...], lam_init).astype(o_ref.dtype)


def diff_attn_prompt(z, kb, vb, slopes, lamp, subln_g, *, batch, seq, lam_init, q_col0, tq):
    nq = seq // tq
    grid_spec = pltpu.PrefetchScalarGridSpec(
        num_scalar_prefetch=1,
        grid=(batch, DA_HEADS, nq),
        in_specs=[
            pl.BlockSpec((4, DA_QK_DIM), lambda b, h, i, s: (0, 0)),
            pl.BlockSpec((tq, DA_V_DIM), lambda b, h, i, s: (b * nq + i, q_col0 + h)),
            pl.BlockSpec((seq, DA_V_DIM), lambda b, h, i, s: (b, h)),
            pl.BlockSpec((seq, DA_V_DIM), lambda b, h, i, s: (b, h)),
            pl.BlockSpec((1, DA_V_DIM), lambda b, h, i, s: (0, 0)),
        ],
        out_specs=pl.BlockSpec((tq, DA_V_DIM), lambda b, h, i, s: (b * nq + i, h)),
    )
    return pl.pallas_call(
        functools.partial(_diff_attn_kernel, tq=tq, lam_init=lam_init),
        out_shape=jax.ShapeDtypeStruct((batch * seq, DA_WIDTH), BF16),
        grid_spec=grid_spec,
        compiler_params=_cparams(3),
        name="diff_attn_prompt",
    )(slopes, lamp, z, kb, vb, subln_g.reshape(1, -1))


N_QROWS = DA_HEADS * 2 * 4


def _paged_attn_kernel(pt_ref, lamp_ref, slope_ref, q_ref, kn_ref, vn_ref, g_ref, *rest, pages_per_step, t_new,
                       past_len, lam_init):
    pp = pages_per_step
    k_refs = rest[:pp]
    v_refs = rest[pp:2 * pp]
    o_ref = rest[2 * pp]
    qz_scr, m_scr, l_scr, acc_scr = rest[2 * pp + 1:]
    g = pl.program_id(1)
    scale = DA_QK_DIM ** -0.5

    rowi = lax.broadcasted_iota(jnp.int32, (N_QROWS, 1), 0)
    slope = slope_ref[...]
    tok = rowi % t_new

    @pl.when(g == 0)
    def _():
        row = lax.broadcasted_iota(jnp.int32, (N_QROWS, DA_WIDTH), 0)
        col = lax.broadcasted_iota(jnp.int32, (N_QROWS, DA_WIDTH), 1)
        keep = (col // DA_QK_DIM) == (row // t_new)
        qz_scr[...] = jnp.where(keep, q_ref[...] * scale, 0.0).astype(BF16)
        m_scr[...] = jnp.full(m_scr.shape, NEG_INF, F32)
        l_scr[...] = jnp.zeros(l_scr.shape, F32)
        acc_scr[...] = jnp.zeros(acc_scr.shape, F32)

    qz = qz_scr[...]
    s = jnp.concatenate([_dot_nt(qz, k_refs[r][...].astype(BF16)) for r in range(pp)], axis=1)
    kpos = g * (pp * PAGE_SIZE) + lax.broadcasted_iota(jnp.int32, (1, pp * PAGE_SIZE), 1)
    dist = ((past_len + tok) - kpos).astype(F32)
    s = s - slope * dist
    m_old = m_scr[...]
    m_new = jnp.maximum(m_old, jnp.max(s, axis=-1, keepdims=True))
    alpha = jnp.exp(m_old - m_new)
    p = jnp.exp(s - m_new)
    l_scr[...] = alpha * l_scr[...] + jnp.sum(p, axis=-1, keepdims=True)
    pv = _dot(p[:, :PAGE_SIZE].astype(BF16), v_refs[0][...].astype(BF16))
    for r in range(1, pp):
        pv += _dot(p[:, r * PAGE_SIZE:(r + 1) * PAGE_SIZE].astype(BF16), v_refs[r][...].astype(BF16))
    acc_scr[...] = alpha * acc_scr[...] + pv
    m_scr[...] = m_new

    @pl.when(g == pl.num_programs(1) - 1)
    def _():
        qf = qz.astype(F32)
        kn = kn_ref[...].astype(BF16).astype(F32)
        vn = vn_ref[...].astype(BF16).astype(F32)
        sc = []
        for j in range(t_new):
            sj = jnp.sum(qf * kn[j:j + 1, :], axis=-1, keepdims=True) - slope * (tok - j).astype(F32)
            sc.append(jnp.where(tok >= j, sj, NEG_INF))
        m_old = m_scr[...]
        m_new = m_old
        for sj in sc:
            m_new = jnp.maximum(m_new, sj)
        alpha = jnp.exp(m_old - m_new)
        l = alpha * l_scr[...]
        acc = alpha * acc_scr[...]
        for j in range(t_new):
            pj = jnp.exp(sc[j] - m_new)
            l = l + pj
            acc = acc + pj.astype(BF16).astype(F32) * vn[j:j + 1, :]
        o = acc / l
        lam = _diff_lambda(lamp_ref[...], lam_init)
        outs = []
        for h in range(DA_HEADS):
            cols = slice(h * DA_V_DIM, (h + 1) * DA_V_DIM)
            r0 = h * 2 * t_new
            oh = o[r0:r0 + t_new, cols] - lam * o[r0 + t_new:r0 + 2 * t_new, cols]
            outs.append(_sub_norm(oh, g_ref[...], lam_init))
        o_ref[...] = jnp.concatenate(outs, axis=1)


def diff_attn_sample(page_table, q_rep, k_new, v_new, cache_k, cache_v, lamp, row_slopes, subln_g, *, layer,
                     past_len, lam_init, pages_per_step):
    n_seq, t_new, _ = k_new.shape
    n_pages = page_table.shape[1]
    pp = pages_per_step

    def page_spec(r):
        return pl.BlockSpec((None, None, PAGE_SIZE, DA_WIDTH), lambda b, g, pt: (layer, pt[b, g * pp + r], 0, 0))

    grid_spec = pltpu.PrefetchScalarGridSpec(
        num_scalar_prefetch=1,
        grid=(n_seq, n_pages // pp),
        in_specs=[
            pl.BlockSpec((4, DA_QK_DIM), lambda b, g, pt: (0, 0)),
            pl.BlockSpec((N_QROWS, 1), lambda b, g, pt: (0, 0)),
            pl.BlockSpec((None, N_QROWS, DA_WIDTH), lambda b, g, pt: (b, 0, 0)),
            pl.BlockSpec((None, t_new, DA_WIDTH), lambda b, g, pt: (b, 0, 0)),
            pl.BlockSpec((None, t_new, DA_WIDTH), lambda b, g, pt: (b, 0, 0)),
            pl.BlockSpec((1, DA_V_DIM), lambda b, g, pt: (0, 0)),
        ] + [page_spec(r) for r in range(pp)] + [page_spec(r) for r in range(pp)],
        out_specs=pl.BlockSpec((None, t_new, DA_WIDTH), lambda b, g, pt: (b, 0, 0)),
        scratch_shapes=[
            pltpu.VMEM((N_QROWS, DA_WIDTH), BF16),
            pltpu.VMEM((N_QROWS, 1), F32),
            pltpu.VMEM((N_QROWS, 1), F32),
            pltpu.VMEM((N_QROWS, DA_WIDTH), F32),
        ],
    )
    return pl.pallas_call(
        functools.partial(_paged_attn_kernel, pages_per_step=pp, t_new=t_new, past_len=past_len, lam_init=lam_init),
        out_shape=jax.ShapeDtypeStruct((n_seq, t_new, DA_WIDTH), F32),
        grid_spec=grid_spec,
        compiler_params=_cparams(2),
        name="diff_attn_sample",
    )(page_table, lamp, row_slopes, q_rep, k_new, v_new, subln_g.reshape(1, -1), *([cache_k] * pp),
      *([cache_v] * pp))


def _mem_attn_kernel(q_ref, mk_ref, mv_ref, o_ref):
    scale = MEM_HEAD_DIM ** -0.5
    q = q_ref[...].astype(BF16)
    mk = mk_ref[...].astype(BF16)
    mv = mv_ref[...].astype(BF16)
    outs = []
    for h in range(MEM_HEADS):
        cols = slice(h * MEM_HEAD_DIM, (h + 1) * MEM_HEAD_DIM)
        s = _dot_nt(q[:, cols], mk[:, cols]) * scale
        e = jnp.exp(s - jnp.max(s, axis=-1, keepdims=True))
        p = e / jnp.sum(e, axis=-1, keepdims=True)
        outs.append(_dot(p.astype(BF16), mv[:, cols]))
    o_ref[...] = jnp.concatenate(outs, axis=1).astype(o_ref.dtype)


def mem_attn(q_arr, mk, mv, *, n_batch, rows_per_batch, tq, q_col0):
    nq = rows_per_batch // tq
    return pl.pallas_call(
        _mem_attn_kernel,
        out_shape=jax.ShapeDtypeStruct((n_batch * rows_per_batch, MEM_WIDTH), BF16),
        grid=(n_batch, nq),
        in_specs=[
            pl.BlockSpec((tq, MEM_WIDTH), lambda b, i: (b * nq + i, q_col0)),
            pl.BlockSpec((None, N_MEM, MEM_WIDTH), lambda b, i: (b, 0, 0)),
            pl.BlockSpec((None, N_MEM, MEM_WIDTH), lambda b, i: (b, 0, 0)),
        ],
        out_specs=pl.BlockSpec((tq, MEM_WIDTH), lambda b, i: (b * nq + i, 0)),
        compiler_params=_cparams(2),
        name="mem_attn",
    )(q_arr, mk, mv)


def _merge_kernel(x_ref, a_ref, b_ref, m_ref, gate_ref, wa_ref, wb_ref, wm_ref, wo_ref, o_ref):
    d = D_MODEL
    ga = gate_ref[:, 0:d]
    gb = gate_ref[:, d:2 * d]
    gm = gate_ref[:, 2 * d:3 * d]
    y = ga.astype(F32) * _dot(a_ref[...].astype(BF16), wa_ref[...])
    y += gb.astype(F32) * _dot(b_ref[...].astype(BF16), wb_ref[...])
    y += gm.astype(F32) * _dot(m_ref[...].astype(BF16), wm_ref[...])
    o_ref[...] = x_ref[...] + _dot(y.astype(BF16), wo_ref[...])


def merge(x, a, b, mo, gates, wa, wb, wm, wo, *, tm):
    m, d = x.shape
    full = lambda arr: pl.BlockSpec(arr.shape, lambda i: (0, 0))
    rowblk = lambda arr: pl.BlockSpec((tm, arr.shape[1]), lambda i: (i, 0))
    return pl.pallas_call(
        _merge_kernel,
        out_shape=jax.ShapeDtypeStruct((m, d), F32),
        grid=(m // tm,),
        in_specs=[rowblk(x), rowblk(a), rowblk(b), rowblk(mo), rowblk(gates), full(wa), full(wb), full(wm), full(wo)],
        out_specs=pl.BlockSpec((tm, d), lambda i: (i, 0)),
        compiler_params=_cparams(1),
        name="merge",
    )(x, a, b, mo, gates, wa, wb, wm, wo)


def _silu(x):
    return x / (1.0 + jnp.exp(-x))


def _ffn_kernel(x_ref, g_ref, wg_ref, wu_ref, wd_ref, o_ref, h_scr, acc_scr):
    f = pl.program_id(1)

    @pl.when(f == 0)
    def _():
        h_scr[...] = _rms(x_ref[...], g_ref[...]).astype(BF16)
        acc_scr[...] = jnp.zeros(acc_scr.shape, F32)

    h = h_scr[...]
    act = _silu(_dot(h, wg_ref[...])) * _dot(h, wu_ref[...])
    acc_scr[...] += _dot(act.astype(BF16), wd_ref[...])

    @pl.when(f == pl.num_programs(1) - 1)
    def _():
        o_ref[...] = x_ref[...] + acc_scr[...]


def ffn_dense(x, g, wg, wu, wd, *, tm, tf):
    m, d = x.shape
    f = wg.shape[1]
    return pl.pallas_call(
        _ffn_kernel,
        out_shape=jax.ShapeDtypeStruct((m, d), F32),
        grid=(m // tm, f // tf),
        in_specs=[
            pl.BlockSpec((tm, d), lambda i, j: (i, 0)),
            pl.BlockSpec((1, d), lambda i, j: (0, 0)),
            pl.BlockSpec((d, tf), lambda i, j: (0, j)),
            pl.BlockSpec((d, tf), lambda i, j: (0, j)),
            pl.BlockSpec((tf, d), lambda i, j: (j, 0)),
        ],
        out_specs=pl.BlockSpec((tm, d), lambda i, j: (i, 0)),
        scratch_shapes=[pltpu.VMEM((tm, d), BF16), pltpu.VMEM((tm, d), F32)],
        compiler_params=_cparams(2),
        name="ffn_dense",
    )(x, g.reshape(1, d), wg, wu, wd)


def _router_kernel(x_ref, g_ref, wr_ref, br_ref, h_ref, idx_ref, gate_ref):
    h = _rms(x_ref[...], g_ref[...])
    h_ref[...] = h.astype(BF16)
    logits = jnp.dot(h, wr_ref[...], preferred_element_type=F32, precision=lax.Precision.HIGHEST) + br_ref[...]
    lane_i = lax.broadcasted_iota(jnp.int32, logits.shape, 1)
    lane = lane_i.astype(F32)
    big = float(ROUTER_LANES)
    t1 = jnp.max(logits, axis=-1, keepdims=True)
    i1 = jnp.min(jnp.where(logits == t1, lane, big), axis=-1, keepdims=True)
    rest = jnp.where(lane == i1, NEG_INF, logits)
    t2 = jnp.max(rest, axis=-1, keepdims=True)
    i2 = jnp.min(jnp.where(rest == t2, lane, big), axis=-1, keepdims=True)
    e2 = jnp.exp(t2 - t1)
    g1 = 1.0 / (1.0 + e2)
    idx_ref[...] = jnp.where(lane_i == 0, i1, i2).astype(jnp.int32)
    gate_ref[...] = jnp.where(lane_i == 0, g1, e2 * g1)


def moe_router(x, g, w_router_pad, b_router_pad, *, tm):
    m, d = x.shape
    wide = pl.BlockSpec((tm, ROUTER_LANES), lambda i: (i, 0))
    return pl.pallas_call(
        _router_kernel,
        out_shape=(jax.ShapeDtypeStruct((m, d), BF16), jax.ShapeDtypeStruct((m, ROUTER_LANES), jnp.int32),
                   jax.ShapeDtypeStruct((m, ROUTER_LANES), F32)),
        grid=(m // tm,),
        in_specs=[
            pl.BlockSpec((tm, d), lambda i: (i, 0)),
            pl.BlockSpec((1, d), lambda i: (0, 0)),
            pl.BlockSpec((d, ROUTER_LANES), lambda i: (0, 0)),
            pl.BlockSpec((1, ROUTER_LANES), lambda i: (0, 0)),
        ],
        out_specs=(pl.BlockSpec((tm, d), lambda i: (i, 0)), wide, wide),
        compiler_params=_cparams(1),
        name="moe_router",
    )(x, g.reshape(1, d), w_router_pad, b_router_pad)


def _moe_ffn_kernel(be_ref, nused_ref, x_ref, wg_ref, wu_ref, wd_ref, o_ref, acc_scr):
    b = pl.program_id(0)
    f = pl.program_id(1)

    @pl.when(b < nused_ref[0])
    def _():
        @pl.when(f == 0)
        def _():
            acc_scr[...] = jnp.zeros(acc_scr.shape, F32)

        x = x_ref[...]
        act = _silu(_dot(x, wg_ref[...])) * _dot(x, wu_ref[...])
        acc_scr[...] += _dot(act.astype(BF16), wd_ref[...])

        @pl.when(f == pl.num_programs(1) - 1)
        def _():
            o_ref[...] = acc_scr[...].astype(o_ref.dtype)

    @pl.when(jnp.logical_and(b >= nused_ref[0], f == pl.num_programs(1) - 1))
    def _():
        o_ref[...] = jnp.zeros(o_ref.shape, o_ref.dtype)


def moe_ffn_grouped(block_expert, n_used, x_buf, wg, wu, wd, *, bm, tf):
    l, d = x_buf.shape
    n_f = wg.shape[2] // tf

    def fidx(b, f, nused):
        return jnp.where(b < nused[0], f, n_f - 1)

    grid_spec = pltpu.PrefetchScalarGridSpec(
        num_scalar_prefetch=2,
        grid=(l // bm, n_f),
        in_specs=[
            pl.BlockSpec((bm, d), lambda b, f, be, nu: (b, 0)),
            pl.BlockSpec((None, d, tf), lambda b, f, be, nu: (be[b], 0, fidx(b, f, nu))),
            pl.BlockSpec((None, d, tf), lambda b, f, be, nu: (be[b], 0, fidx(b, f, nu))),
            pl.BlockSpec((None, tf, d), lambda b, f, be, nu: (be[b], fidx(b, f, nu), 0)),
        ],
        out_specs=pl.BlockSpec((bm, d), lambda b, f, be, nu: (b, 0)),
        scratch_shapes=[pltpu.VMEM((bm, d), F32)],
    )
    return pl.pallas_call(
        _moe_ffn_kernel,
        out_shape=jax.ShapeDtypeStruct((l, d), BF16),
        grid_spec=grid_spec,
        compiler_params=_cparams(2),
        name="moe_ffn_grouped",
    )(block_expert, n_used, x_buf, wg, wu, wd)


def _combine_norm_kernel(x_ref, y0_ref, y1_ref, gate_ref, g_ref, xo_ref, yo_ref):
    gate = gate_ref[...]
    x = x_ref[...] + gate[:, 0:1] * y0_ref[...].astype(F32) + gate[:, 1:2] * y1_ref[...].astype(F32)
    xo_ref[...] = x
    yo_ref[...] = _rms(x, g_ref[...])


def combine_norm(x, y0, y1, gates, g_final, *, tm):
    m, d = x.shape
    blk = pl.BlockSpec((tm, d), lambda i: (i, 0))
    return pl.pallas_call(
        _combine_norm_kernel,
        out_shape=(jax.ShapeDtypeStruct((m, d), F32), jax.ShapeDtypeStruct((m, d), F32)),
        grid=(m // tm,),
        in_specs=[blk, blk, blk, pl.BlockSpec((tm, ROUTER_LANES), lambda i: (i, 0)),
                  pl.BlockSpec((1, d), lambda i: (0, 0))],
        out_specs=(blk, blk),
        compiler_params=_cparams(1),
        name="combine_norm",
    )(x, y0, y1, gates, g_final.reshape(1, d))


def moe_layer(x, g, w_router, b_router, wg, wu, wd, g_final, *, tm, bm, tf):
    m, d = x.shape
    wr = jnp.zeros((d, ROUTER_LANES), F32).at[:, :N_EXPERTS].set(w_router)
    br = jnp.full((1, ROUTER_LANES), NEG_INF, F32).at[0, :N_EXPERTS].set(b_router)
    h, idx, gates = moe_router(x, g, wr, br, tm=tm)

    nk = m * TOP_K
    flat_e = idx[:, :TOP_K].reshape(nk)
    onehot = (flat_e[:, None] == jnp.arange(N_EXPERTS, dtype=jnp.int32)[None, :]).astype(jnp.int32)
    csum = jnp.cumsum(onehot, axis=0)
    counts = csum[-1]
    rank = jnp.take_along_axis(csum, flat_e[:, None], axis=1)[:, 0] - 1
    padded = (counts + bm - 1) // bm * bm
    pad_end = jnp.cumsum(padded)
    pad_start = pad_end - padded
    dest = pad_start[flat_e] + rank
    l_rows = (nk + bm - 1) // bm * bm + N_EXPERTS * bm
    n_blocks = l_rows // bm
    tok_buf = jnp.zeros((l_rows,), jnp.int32).at[dest].set(jnp.arange(nk, dtype=jnp.int32) // TOP_K)
    block_expert = jnp.clip(
        jnp.searchsorted(pad_end, jnp.arange(n_blocks, dtype=jnp.int32) * bm, side="right"), 0, N_EXPERTS - 1
    ).astype(jnp.int32)
    n_used = (pad_end[-1] // bm).astype(jnp.int32).reshape(1)

    x_buf = jnp.take(h, tok_buf, axis=0)
    y_buf = moe_ffn_grouped(block_expert, n_used, x_buf, wg, wu, wd, bm=bm, tf=tf)
    dest2 = dest.reshape(m, TOP_K)
    y0 = jnp.take(y_buf, dest2[:, 0], axis=0)
    y1 = jnp.take(y_buf, dest2[:, 1], axis=0)
    return combine_norm(x, y0, y1, gates, g_final, tm=tm)


def _sample_mix_weights(sgu_w, sgu_b, t_new, n_seq):
    r = jnp.arange(n_seq * t_new)
    t = r % t_new
    same = (r[:, None] // t_new) == (r[None, :] // t_new)
    causal = t[None, :] <= t[:, None]
    w = sgu_w[:, t[:, None], t[None, :]] * (same & causal)[None].astype(sgu_w.dtype)
    bias = jnp.repeat(sgu_b[:, t].T, CHUNK, axis=1)
    return w, bias


def kernel(x_prompt, x_sample, mem_prompt, cache_attn_k, cache_attn_v, cache_mem_k, cache_mem_v, page_table, norm_mix, w_in, b_gate, sgu_ln_g, sgu_ln_b, sgu_w, sgu_b, lam_q1, lam_k1, lam_q2, lam_k2, subln_g, mem_norm, w_mem_k, w_mem_v, w_br_a, w_br_b, w_br_m, w_out, norm_ffn, w_d_gate, w_d_up, w_d_down, w_router, b_router, w_e_gate, w_e_up, w_e_down, norm_final):
    batch, seq, d = x_prompt.shape
    n_seq, t_new, _ = x_sample.shape
    depth = w_in.shape[0]
    n_pages = page_table.shape[1]
    past_len = n_pages * PAGE_SIZE
    mp = batch * seq
    ms = n_seq * t_new
    assert ms == CHUNK and d == D_MODEL

    xp = x_prompt.reshape(mp, d)
    xs = x_sample.reshape(ms, d)
    mem = mem_prompt.reshape(batch * N_MEM, d)
    ck = cache_attn_k.reshape(cache_attn_k.shape[0], cache_attn_k.shape[1], PAGE_SIZE, DA_WIDTH)
    cv = cache_attn_v.reshape(cache_attn_v.shape[0], cache_attn_v.shape[1], PAGE_SIZE, DA_WIDTH)
    slopes = jnp.asarray([2.0 ** (-8.0 * (i + 1) / DA_HEADS) for i in range(DA_HEADS)], F32)
    tril = jnp.tril(jnp.ones((CHUNK, CHUNK), F32))
    row_slopes = jnp.repeat(slopes, 2 * t_new).reshape(N_QROWS, 1)

    c_q = 2 * SGU_WIDTH
    c_k = c_q + DA_WIDTH
    c_qm = c_k + 2 * DA_WIDTH
    c_gate = c_qm + MEM_WIDTH

    outs = {k: [] for k in ("pk", "pv", "pmk", "pmv", "sk", "sv", "scv")}
    yp = ys = None
    for l in range(depth):
        lam_init = 0.8 - 0.6 * math.exp(-0.3 * l)
        lamp = jnp.stack([lam_q1[l], lam_k1[l], lam_q2[l], lam_k2[l]])
        w_l = w_in[l]
        w_z = jnp.concatenate([w_l[:, :c_k], w_l[:, c_qm:c_gate]], axis=1).astype(BF16)
        w_kv = w_l[:, c_k:c_qm].astype(BF16)
        w_g = w_l[:, c_gate:].astype(BF16)
        zero_bias = jnp.zeros((w_z.shape[1],), F32)
        w_mem = jnp.concatenate([w_mem_k[l], w_mem_v[l]], axis=1).astype(BF16)
        wa, wb, wm, wo = (w.astype(BF16) for w in (w_br_a[l], w_br_b[l], w_br_m[l], w_out[l]))
        w_mix_p = (sgu_w[l] * tril[None]).astype(BF16)
        bias_p = jnp.repeat(sgu_b[l].T, CHUNK, axis=1)
        w_mix_s, bias_s = _sample_mix_weights(sgu_w[l], sgu_b[l], t_new, n_seq)
        qm_blk = (c_k) // MEM_WIDTH

        mkv = norm_matmul(mem, mem_norm[l], w_mem, jnp.zeros((2 * MEM_WIDTH,), F32), sigmoid=False,
                          out_dtype=F32, tm=batch * N_MEM, tn=MEM_WIDTH)
        mk_p = mkv[:, :MEM_WIDTH].reshape(batch, N_MEM, MEM_WIDTH)
        mv_p = mkv[:, MEM_WIDTH:].reshape(batch, N_MEM, MEM_WIDTH)

        z = norm_matmul(xp, norm_mix[l], w_z, zero_bias, sigmoid=False, out_dtype=BF16, tm=1024, tn=512)
        kf, vf, kb, vb = norm_kv(xp, norm_mix[l], w_kv, tm=512)
        gates = norm_matmul(xp, norm_mix[l], w_g, b_gate[l], sigmoid=True, out_dtype=BF16, tm=1024, tn=512)
        a, _ = sgu(z, sgu_ln_g[l], sgu_ln_b[l], w_mix_p, bias_p, n_chunks=4)
        bo = diff_attn_prompt(z, kb, vb, slopes, lamp, subln_g[l], batch=batch, seq=seq, lam_init=lam_init,
                              q_col0=c_q // DA_V_DIM, tq=512)
        mo = mem_attn(z, mk_p, mv_p, n_batch=batch, rows_per_batch=seq, tq=512, q_col0=qm_blk)
        xp = merge(xp, a, bo, mo, gates, wa, wb, wm, wo, tm=512)

        zs = norm_matmul(xs, norm_mix[l], w_z, zero_bias, sigmoid=False, out_dtype=BF16, tm=ms, tn=512)
        kfs, vfs, _, _ = norm_kv(xs, norm_mix[l], w_kv, tm=ms)
        gates_s = norm_matmul(xs, norm_mix[l], w_g, b_gate[l], sigmoid=True, out_dtype=BF16, tm=ms, tn=512)
        a_s, vrows = sgu(zs, sgu_ln_g[l], sgu_ln_b[l], w_mix_s.astype(BF16), bias_s, n_chunks=1)
        q_s = zs[:, c_q:c_k].astype(F32).reshape(n_seq, 1, t_new, DA_WIDTH)
        q_rep = jnp.broadcast_to(q_s, (n_seq, 2 * DA_HEADS, t_new, DA_WIDTH)).reshape(n_seq, N_QROWS, DA_WIDTH)
        bo_s = diff_attn_sample(page_table, q_rep, kfs.reshape(n_seq, t_new, DA_WIDTH),
                                vfs.reshape(n_seq, t_new, DA_WIDTH), ck, cv, lamp, row_slopes, subln_g[l], layer=l,
                                past_len=past_len, lam_init=lam_init, pages_per_step=8)
        qm_s = zs[:, c_k:c_k + MEM_WIDTH].reshape(n_seq, t_new, MEM_WIDTH)
        qm_s = jnp.pad(qm_s, ((0, 0), (0, QM_PAD_ROWS - t_new), (0, 0))).reshape(n_seq * QM_PAD_ROWS, MEM_WIDTH)
        mo_s = mem_attn(qm_s, cache_mem_k[l].reshape(n_seq, N_MEM, MEM_WIDTH),
                        cache_mem_v[l].reshape(n_seq, N_MEM, MEM_WIDTH), n_batch=n_seq,
                        rows_per_batch=QM_PAD_ROWS, tq=QM_PAD_ROWS, q_col0=0)
        mo_s = mo_s.reshape(n_seq, QM_PAD_ROWS, MEM_WIDTH)[:, :t_new].reshape(ms, MEM_WIDTH)
        xs = merge(xs, a_s, bo_s.reshape(ms, DA_WIDTH), mo_s, gates_s, wa, wb, wm, wo, tm=ms)

        if l % 2 == 0:
            i = l // 2
            wg, wu, wd = w_d_gate[i].astype(BF16), w_d_up[i].astype(BF16), w_d_down[i].astype(BF16)
            xp = ffn_dense(xp, norm_ffn[l], wg, wu, wd, tm=512, tf=1408)
            xs = ffn_dense(xs, norm_ffn[l], wg, wu, wd, tm=ms, tf=1408)
        else:
            i = l // 2
            wg, wu, wd = w_e_gate[i].astype(BF16), w_e_up[i].astype(BF16), w_e_down[i].astype(BF16)
            xp, yp = moe_layer(xp, norm_ffn[l], w_router[i], b_router[i], wg, wu, wd, norm_final, tm=512, bm=512,
                               tf=512)
            xs, ys = moe_layer(xs, norm_ffn[l], w_router[i], b_router[i], wg, wu, wd, norm_final, tm=ms, bm=128,
                               tf=512)

        outs["pk"].append(kf.reshape(batch, seq, DA_HEADS, DA_V_DIM))
        outs["pv"].append(vf.reshape(batch, seq, DA_HEADS, DA_V_DIM))
        outs["pmk"].append(mk_p.reshape(batch, N_MEM, MEM_HEADS, MEM_HEAD_DIM))
        outs["pmv"].append(mv_p.reshape(batch, N_MEM, MEM_HEADS, MEM_HEAD_DIM))
        outs["sk"].append(kfs.reshape(n_seq, t_new, DA_HEADS, DA_V_DIM))
        outs["sv"].append(vfs.reshape(n_seq, t_new, DA_HEADS, DA_V_DIM))
        outs["scv"].append(vrows.reshape(n_seq, t_new, SGU_WIDTH))

    if depth % 2 == 1:
        raise NotImplementedError("final norm is fused into the expert layer, which must come last")
    return (yp.reshape(batch, seq, d), ys.reshape(n_seq, t_new, d),
            jnp.stack(outs["pk"]), jnp.stack(outs["pv"]), jnp.stack(outs["pmk"]), jnp.stack(outs["pmv"]),
            jnp.stack(outs["sk"]), jnp.stack(outs["sv"]), jnp.stack(outs["scv"]))
```

```python
import functools
import math

import jax
import jax.numpy as jnp
from jax import lax
from jax.experimental import pallas as pl
from jax.experimental.pallas import tpu as pltpu

F32 = jnp.float32
BF16 = jnp.bfloat16

D_MODEL = 1024
CHUNK = 128
SGU_GROUPS = 4
SGU_WIDTH = 512
DA_HEADS = 8
DA_QK_DIM = 64
DA_V_DIM = 128
DA_WIDTH = DA_HEADS * DA_V_DIM
MEM_HEADS = 4
MEM_HEAD_DIM = 128
MEM_WIDTH = MEM_HEADS * MEM_HEAD_DIM
N_MEM = 256
N_EXPERTS = 8
TOP_K = 2
PAGE_SIZE = 128
EPS = 1e-5
NEG_INF = float("-inf")

VMEM_LIMIT_BYTES = 48 * 1024 * 1024
LANES = 128
ROUTER_LANES = 128
QM_PAD_ROWS = 16
ALIBI_SPLIT = 16


def _cparams(n_axes):
    return pltpu.CompilerParams(
        dimension_semantics=("arbitrary",) * n_axes, vmem_limit_bytes=VMEM_LIMIT_BYTES)


def _rms(x, g):
    ms = jnp.mean(x * x, axis=-1, keepdims=True)
    return x * lax.rsqrt(ms + EPS) * g


def _gelu(x):
    return 0.5 * x * (1.0 + lax.erf(x * (1.0 / math.sqrt(2.0))))


def _dot(a, b):
    return jnp.dot(a, b, preferred_element_type=F32)


def _dot_nt(a, b):
    return lax.dot_general(a, b, (((1,), (1,)), ((), ())), preferred_element_type=F32)


def _norm_matmul_kernel(x_ref, g_ref, w_ref, b_ref, o_ref, h_scr, *, sigmoid):
    @pl.when(pl.program_id(1) == 0)
    def _():
        h_scr[...] = _rms(x_ref[...], g_ref[...]).astype(BF16)

    acc = _dot(h_scr[...], w_ref[...])
    if sigmoid:
        acc = 1.0 / (1.0 + jnp.exp(-(acc + b_ref[...])))
    o_ref[...] = acc.astype(o_ref.dtype)


def norm_matmul(x, g, w, bias, *, sigmoid, out_dtype, tm, tn):
    m, d = x.shape
    n = w.shape[1]
    return pl.pallas_call(
        functools.partial(_norm_matmul_kernel, sigmoid=sigmoid),
        out_shape=jax.ShapeDtypeStruct((m, n), out_dtype),
        grid=(m // tm, n // tn),
        in_specs=[
            pl.BlockSpec((tm, d), lambda i, j: (i, 0)),
            pl.BlockSpec((1, d), lambda i, j: (0, 0)),
            pl.BlockSpec((d, tn), lambda i, j: (0, j)),
            pl.BlockSpec((1, tn), lambda i, j: (0, j)),
        ],
        out_specs=pl.BlockSpec((tm, tn), lambda i, j: (i, j)),
        scratch_shapes=[pltpu.VMEM((tm, d), BF16)],
        compiler_params=_cparams(2),
        name="norm_matmul_sig" if sigmoid else "norm_matmul",
    )(x, g.reshape(1, d), w, bias.reshape(1, n))


def _norm_kv_kernel(x_ref, g_ref, w_ref, kf_ref, vf_ref, kb_ref, vt_ref, h_scr):
    j = pl.program_id(1)

    @pl.when(j == 0)
    def _():
        h_scr[...] = _rms(x_ref[...], g_ref[...]).astype(BF16)

    acc = _dot(h_scr[...], w_ref[...])

    @pl.when(j == 0)
    def _():
        kf_ref[...] = acc
        kb_ref[...] = acc.astype(BF16)

    @pl.when(j == 1)
    def _():
        vf_ref[...] = acc
        vt_ref[...] = acc.T.astype(BF16)


def norm_kv(x, g, w_kv, *, tm, batch):
    m, d = x.shape
    n = w_kv.shape[1] // 2
    nb = m // batch // tm
    blk = pl.BlockSpec((tm, n), lambda i, j: (i, 0))
    return pl.pallas_call(
        _norm_kv_kernel,
        out_shape=(jax.ShapeDtypeStruct((m, n), F32), jax.ShapeDtypeStruct((m, n), F32),
                   jax.ShapeDtypeStruct((m, n), BF16), jax.ShapeDtypeStruct((batch, n, m // batch), BF16)),
        grid=(m // tm, 2),
        in_specs=[
            pl.BlockSpec((tm, d), lambda i, j: (i, 0)),
            pl.BlockSpec((1, d), lambda i, j: (0, 0)),
            pl.BlockSpec((d, n), lambda i, j: (0, j)),
        ],
        out_specs=(blk, blk, blk, pl.BlockSpec((None, n, tm), lambda i, j: (i // nb, 0, i % nb))),
        scratch_shapes=[pltpu.VMEM((tm, d), BF16)],
        compiler_params=_cparams(2),
        name="norm_kv",
    )(x, g.reshape(1, d), w_kv)


def _sgu_kernel(u_ref, v_ref, lng_ref, lnb_ref, w_ref, bias_ref, a_ref, *maybe_vn_ref, n_chunks):
    u = _gelu(u_ref[...].astype(F32))
    v = _gelu(v_ref[...].astype(F32))
    mu = jnp.mean(v, axis=-1, keepdims=True)
    vc = v - mu
    var = jnp.mean(vc * vc, axis=-1, keepdims=True)
    vn = vc * lax.rsqrt(var + EPS) * lng_ref[...] + lnb_ref[...]
    if maybe_vn_ref:
        maybe_vn_ref[0][...] = vn
    vnb = vn.astype(BF16)
    bias = bias_ref[...]
    for c in range(n_chunks):
        rows = slice(c * CHUNK, (c + 1) * CHUNK)
        for g in range(SGU_GROUPS):
            cols = slice(g * CHUNK, (g + 1) * CHUNK)
            mixed = _dot(w_ref[g], vnb[rows, cols]) + bias[:, cols]
            a_ref[rows, cols] = (u[rows, cols] * mixed).astype(a_ref.dtype)


def sgu(z, ln_g, ln_b, w_mix, bias_full, *, n_chunks, emit_v):
    m = z.shape[0]
    rows = n_chunks * CHUNK
    row_blk = pl.BlockSpec((rows, SGU_WIDTH), lambda i: (i, 0))
    out_shape = [jax.ShapeDtypeStruct((m, SGU_WIDTH), BF16)]
    out_specs = [row_blk]
    if emit_v:
        out_shape.append(jax.ShapeDtypeStruct((m, SGU_WIDTH), F32))
        out_specs.append(row_blk)
    return pl.pallas_call(
        functools.partial(_sgu_kernel, n_chunks=n_chunks),
        out_shape=tuple(out_shape),
        grid=(m // rows,),
        in_specs=[
            row_blk,
            pl.BlockSpec((rows, SGU_WIDTH), lambda i: (i, 1)),
            pl.BlockSpec((1, SGU_WIDTH), lambda i: (0, 0)),
            pl.BlockSpec((1, SGU_WIDTH), lambda i: (0, 0)),
            pl.BlockSpec((SGU_GROUPS, CHUNK, CHUNK), lambda i: (0, 0, 0)),
            pl.BlockSpec((CHUNK, SGU_WIDTH), lambda i: (0, 0)),
        ],
        out_specs=tuple(out_specs),
        compiler_params=_cparams(1),
        name="sgu",
    )(z, z, ln_g.reshape(1, -1), ln_b.reshape(1, -1), w_mix, bias_full)


def _diff_lambda(lamp, lam_init):
    s1 = jnp.sum(lamp[0:1, :] * lamp[1:2, :], axis=-1, keepdims=True)
    s2 = jnp.sum(lamp[2:3, :] * lamp[3:4, :], axis=-1, keepdims=True)
    return jnp.exp(s1) - jnp.exp(s2) + lam_init


def _sub_norm(o, g, lam_init):
    return _rms(o, g) * (1.0 - lam_init)


def _diff_attn_kernel(slopes_ref, lamp_ref, q_ref, k_ref, vt_ref, g_ref, o_ref, *, tq, lam_init):
    h = pl.program_id(1)
    qi = pl.program_id(2)
    slope = slopes_ref[h]

    q = q_ref[...] * jnp.asarray(DA_QK_DIM ** -0.5, BF16)
    lane = lax.broadcasted_iota(jnp.int32, (tq, LANES), 1)
    pos = lax.broadcasted_iota(jnp.int32, (tq, LANES), 0)
    hi = (pos // ALIBI_SPLIT).astype(F32)
    lo = (pos % ALIBI_SPLIT).astype(F32)
    zero = jnp.zeros((tq, LANES), F32)
    q_alibi = jnp.where(lane == 0, slope * ALIBI_SPLIT, jnp.where(lane == 1, slope, jnp.where(
        lane == 2, -slope * ALIBI_SPLIT * hi, jnp.where(lane == 3, -slope * lo, zero)))).astype(BF16)
    k_alibi = jnp.where(lane == 0, hi, jnp.where(lane == 1, lo, jnp.where(lane <= 3, 1.0, zero))).astype(BF16)
    qzero = jnp.zeros_like(q)
    qz = (jnp.concatenate([jnp.where(lane < DA_QK_DIM, q, qzero), q_alibi], axis=1),
          jnp.concatenate([jnp.where(lane >= DA_QK_DIM, q, qzero), q_alibi], axis=1))

    krow = lax.broadcasted_iota(jnp.int32, (tq, tq), 0)
    qcol = lax.broadcasted_iota(jnp.int32, (tq, tq), 1)
    causal = jnp.where(krow <= qcol, 0.0, NEG_INF)

    def update(carry, kt, vt, mask, shift):
        kt = jnp.concatenate([kt, k_alibi], axis=1)
        ts = [_dot_nt(kt, qm) for qm in qz]
        new = []
        for (m, l, acc), t in zip(carry, ts):
            if mask is not None:
                t = t + mask
            m_new = jnp.maximum(m, jnp.max(t, axis=0, keepdims=True) + shift)
            alpha = jnp.exp(m - m_new)
            p = jnp.exp(t - (m_new - shift))
            l_new = alpha * l + jnp.sum(p, axis=0, keepdims=True)
            acc_new = alpha * acc + _dot(vt, p.astype(BF16))
            new.append((m_new, l_new, acc_new))
        return tuple(new)

    def body(j, carry):
        start = pl.multiple_of(j * tq, tq)
        shift = ((j - qi) * tq).astype(F32) * slope
        return update(carry, k_ref[pl.ds(start, tq), :], vt_ref[:, pl.ds(start, tq)], None, shift)

    one = (jnp.full((1, tq), NEG_INF, F32), jnp.zeros((1, tq), F32), jnp.zeros((DA_V_DIM, tq), F32))
    carry = lax.fori_loop(0, qi, body, (one, one))
    start = pl.multiple_of(qi * tq, tq)
    (_, l1, acc1), (_, l2, acc2) = update(carry, k_ref[pl.ds(start, tq), :], vt_ref[:, pl.ds(start, tq)], causal, 0.0)
    lam = _diff_lambda(lamp_ref[...], lam_init)
    o = (acc1 / l1 - lam * (acc2 / l2)).T
    o_ref[...] = _sub_norm(o, g_ref[...], lam_init).astype(o_ref.dtype)


def diff_attn_prompt(z, kb, vt, slopes, lamp, subln_g, *, batch, seq, lam_init, q_col0, tq):
    nq = seq // tq
    grid_spec = pltpu.PrefetchScalarGridSpec(
        num_scalar_prefetch=1,
        grid=(batch, DA_HEADS, nq),
        in_specs=[
            pl.BlockSpec((4, DA_QK_DIM), lambda b, h, i, s: (0, 0)),
            pl.BlockSpec((tq, DA_V_DIM), lambda b, h, i, s: (b * nq + i, q_col0 + h)),
            pl.BlockSpec((seq, DA_V_DIM), lambda b, h, i, s: (b, h)),
            pl.BlockSpec((None, DA_V_DIM, seq), lambda b, h, i, s: (b, h, 0)),
            pl.BlockSpec((1, DA_V_DIM), lambda b, h, i, s: (0, 0)),
        ],
        out_specs=pl.BlockSpec((tq, DA_V_DIM), lambda b, h, i, s: (b * nq + i, h)),
    )
    return pl.pallas_call(
        functools.partial(_diff_attn_kernel, tq=tq, lam_init=lam_init),
        out_shape=jax.ShapeDtypeStruct((batch * seq, DA_WIDTH), BF16),
        grid_spec=grid_spec,
        compiler_params=_cparams(3),
        name="diff_attn_prompt",
    )(slopes, lamp, z, kb, vt, subln_g.reshape(1, -1))


N_QROWS = DA_HEADS * 2 * 4


def _paged_attn_kernel(pt_ref, lamp_ref, slope_ref, q_ref, kn_ref, vn_ref, g_ref, *rest, pages_per_step, t_new,
                       past_len, lam_init):
    pp = pages_per_step
    k_refs = rest[:pp]
    v_refs = rest[pp:2 * pp]
    o_ref = rest[2 * pp]
    qz_scr, s_scr, sn_scr, m_scr, l_scr, acc_scr = rest[2 * pp + 1:]
    phase = pl.program_id(1)
    g = pl.program_id(2)
    last = pl.num_programs(2) - 1
    step_keys = pp * PAGE_SIZE
    rows_per_head = 2 * t_new

    rowi = lax.broadcasted_iota(jnp.int32, (N_QROWS, 1), 0)
    slope = slope_ref[...]
    tok = rowi % t_new
    lam = _diff_lambda(lamp_ref[...], lam_init)

    def page(ref):
        return jnp.concatenate(
            [ref[pl.ds(h, PAGE_SIZE, stride=DA_HEADS), :].astype(BF16) for h in range(DA_HEADS)], axis=1)

    def combine(p):
        return pltpu.roll(p, t_new, axis=0) - lam * p

    def new_key_scores():
        kn = jnp.concatenate([kn_ref[...].astype(BF16), jnp.zeros((PAGE_SIZE - t_new, DA_WIDTH), BF16)], axis=0)
        s_all = _dot_nt(qz_scr[...], kn)
        row = lax.broadcasted_iota(jnp.int32, s_all.shape, 0)
        j = lax.broadcasted_iota(jnp.int32, s_all.shape, 1)
        valid = (j < t_new) & (j <= row % t_new)
        return jnp.where(valid, s_all - slope * (row % t_new - j).astype(F32), NEG_INF)

    @pl.when(jnp.logical_and(phase == 0, g == 0))
    def _():
        row = lax.broadcasted_iota(jnp.int32, (N_QROWS, DA_WIDTH), 0)
        lane = lax.broadcasted_iota(jnp.int32, (N_QROWS, DA_WIDTH), 1)
        keep = (lane // DA_QK_DIM) == (row // t_new)
        qz_scr[...] = jnp.where(keep, q_ref[...] * (DA_QK_DIM ** -0.5), 0.0).astype(BF16)
        m_scr[...] = jnp.full(m_scr.shape, NEG_INF, F32)
        l_scr[...] = jnp.zeros(l_scr.shape, F32)
        acc_scr[...] = jnp.zeros(acc_scr.shape, F32)

    def online(s):
        m_old = m_scr[...]
        m_new = jnp.maximum(m_old, jnp.max(s, axis=-1, keepdims=True))
        l_scr[...] = jnp.exp(m_old - m_new) * l_scr[...] + jnp.sum(jnp.exp(s - m_new), axis=-1, keepdims=True)
        m_scr[...] = m_new

    @pl.when(phase == 0)
    def _():
        qz = qz_scr[...]
        s = jnp.concatenate([_dot_nt(qz, page(k_refs[r])) for r in range(pp)], axis=1)
        kpos = g * step_keys + lax.broadcasted_iota(jnp.int32, (1, step_keys), 1)
        s = s - slope * ((past_len + tok) - kpos).astype(F32)
        s_scr[:, pl.ds(pl.multiple_of(g * step_keys, step_keys), step_keys)] = s
        online(s)

        @pl.when(g == last)
        def _():
            sn = new_key_scores()
            sn_scr[...] = sn
            online(sn)

    @pl.when(phase == 1)
    def _():
        s = s_scr[:, pl.ds(pl.multiple_of(g * step_keys, step_keys), step_keys)]
        m = m_scr[...]
        inv_l = 1.0 / l_scr[...]
        pc = combine(jnp.exp(s - m) * inv_l).astype(BF16)
        pv = _dot(pc[:, 0:PAGE_SIZE], page(v_refs[0]))
        for r in range(1, pp):
            pv += _dot(pc[:, r * PAGE_SIZE:(r + 1) * PAGE_SIZE], page(v_refs[r]))
        acc_scr[...] += pv

        @pl.when(g == last)
        def _():
            pn = combine(jnp.exp(sn_scr[...] - m) * inv_l).astype(BF16)
            vn = jnp.concatenate([vn_ref[...].astype(BF16), jnp.zeros((PAGE_SIZE - t_new, DA_WIDTH), BF16)], axis=0)
            acc = acc_scr[...] + _dot(pn, vn)
            outs = []
            for h in range(DA_HEADS):
                r0 = h * rows_per_head + t_new
                outs.append(_sub_norm(acc[r0:r0 + t_new, h * DA_V_DIM:(h + 1) * DA_V_DIM], g_ref[...], lam_init))
            o_ref[...] = jnp.concatenate(outs, axis=1)


def diff_attn_sample(page_table, q_rep, k_new, v_new, cache_k, cache_v, lamp, row_slopes, subln_g, *, layer,
                     past_len, lam_init, pages_per_step):
    n_seq, t_new = k_new.shape[:2]
    n_pages = page_table.shape[1]
    pp = pages_per_step
    n_groups = n_pages // pp
    page_blk = (None, None, PAGE_SIZE * DA_HEADS, DA_V_DIM)

    def k_spec(r):
        return pl.BlockSpec(
            page_blk, lambda b, ph, g, pt: (layer, pt[b, jnp.where(ph == 0, g, n_groups - 1) * pp + r], 0, 0))

    def v_spec(r):
        return pl.BlockSpec(page_blk, lambda b, ph, g, pt: (layer, pt[b, jnp.where(ph == 0, 0, g) * pp + r], 0, 0))

    new_spec = pl.BlockSpec((None, t_new, DA_WIDTH), lambda b, ph, g, pt: (b, 0, 0))
    grid_spec = pltpu.PrefetchScalarGridSpec(
        num_scalar_prefetch=1,
        grid=(n_seq, 2, n_groups),
        in_specs=[
            pl.BlockSpec((4, DA_QK_DIM), lambda b, ph, g, pt: (0, 0)),
            pl.BlockSpec((N_QROWS, 1), lambda b, ph, g, pt: (0, 0)),
            pl.BlockSpec((None, N_QROWS, DA_WIDTH), lambda b, ph, g, pt: (b, 0, 0)),
            new_spec, new_spec,
            pl.BlockSpec((1, DA_V_DIM), lambda b, ph, g, pt: (0, 0)),
        ] + [k_spec(r) for r in range(pp)] + [v_spec(r) for r in range(pp)],
        out_specs=pl.BlockSpec((None, t_new, DA_WIDTH), lambda b, ph, g, pt: (b, 0, 0)),
        scratch_shapes=[
            pltpu.VMEM((N_QROWS, DA_WIDTH), BF16),
            pltpu.VMEM((N_QROWS, n_pages * PAGE_SIZE), F32),
            pltpu.VMEM((N_QROWS, PAGE_SIZE), F32),
            pltpu.VMEM((N_QROWS, 1), F32),
            pltpu.VMEM((N_QROWS, 1), F32),
            pltpu.VMEM((N_QROWS, DA_WIDTH), F32),
        ],
    )
    return pl.pallas_call(
        functools.partial(_paged_attn_kernel, pages_per_step=pp, t_new=t_new, past_len=past_len, lam_init=lam_init),
        out_shape=jax.ShapeDtypeStruct((n_seq, t_new, DA_WIDTH), F32),
        grid_spec=grid_spec,
        compiler_params=_cparams(3),
        name="diff_attn_sample",
    )(page_table, lamp, row_slopes, q_rep, k_new, v_new, subln_g.reshape(1, -1), *([cache_k] * pp),
      *([cache_v] * pp))


def _mem_attn_kernel(q_ref, mk_ref, mv_ref, o_ref):
    scale = MEM_HEAD_DIM ** -0.5
    q = q_ref[...].astype(BF16)
    mk = mk_ref[...].astype(BF16)
    mv = mv_ref[...].astype(BF16)
    outs = []
    for h in range(MEM_HEADS):
        cols = slice(h * MEM_HEAD_DIM, (h + 1) * MEM_HEAD_DIM)
        s = _dot_nt(q[:, cols], mk[:, cols]) * scale
        e = jnp.exp(s - jnp.max(s, axis=-1, keepdims=True))
        p = e / jnp.sum(e, axis=-1, keepdims=True)
        outs.append(_dot(p.astype(BF16), mv[:, cols]))
    o_ref[...] = jnp.concatenate(outs, axis=1).astype(o_ref.dtype)


def mem_attn(q_arr, mk, mv, *, n_batch, rows_per_batch, tq, q_col0):
    nq = rows_per_batch // tq
    return pl.pallas_call(
        _mem_attn_kernel,
        out_shape=jax.ShapeDtypeStruct((n_batch * rows_per_batch, MEM_WIDTH), BF16),
        grid=(n_batch, nq),
        in_specs=[
            pl.BlockSpec((tq, MEM_WIDTH), lambda b, i: (b * nq + i, q_col0)),
            pl.BlockSpec((None, N_MEM, MEM_WIDTH), lambda b, i: (b, 0, 0)),
            pl.BlockSpec((None, N_MEM, MEM_WIDTH), lambda b, i: (b, 0, 0)),
        ],
        out_specs=pl.BlockSpec((tq, MEM_WIDTH), lambda b, i: (b * nq + i, 0)),
        compiler_params=_cparams(2),
        name="mem_attn",
    )(q_arr, mk, mv)


def _merge_kernel(x_ref, a_ref, b_ref, m_ref, gate_ref, wa_ref, wb_ref, wm_ref, wo_ref, o_ref):
    d = D_MODEL
    ga = gate_ref[:, 0:d]
    gb = gate_ref[:, d:2 * d]
    gm = gate_ref[:, 2 * d:3 * d]
    y = ga.astype(F32) * _dot(a_ref[...].astype(BF16), wa_ref[...])
    y += gb.astype(F32) * _dot(b_ref[...].astype(BF16), wb_ref[...])
    y += gm.astype(F32) * _dot(m_ref[...].astype(BF16), wm_ref[...])
    o_ref[...] = x_ref[...] + _dot(y.astype(BF16), wo_ref[...])


def merge(x, a, b, mo, gates, wa, wb, wm, wo, *, tm):
    m, d = x.shape
    full = lambda arr: pl.BlockSpec(arr.shape, lambda i: (0, 0))
    rowblk = lambda arr: pl.BlockSpec((tm, arr.shape[1]), lambda i: (i, 0))
    return pl.pallas_call(
        _merge_kernel,
        out_shape=jax.ShapeDtypeStruct((m, d), F32),
        grid=(m // tm,),
        in_specs=[rowblk(x), rowblk(a), rowblk(b), rowblk(mo), rowblk(gates), full(wa), full(wb), full(wm), full(wo)],
        out_specs=pl.BlockSpec((tm, d), lambda i: (i, 0)),
        compiler_params=_cparams(1),
        name="merge",
    )(x, a, b, mo, gates, wa, wb, wm, wo)


def _silu(x):
    return x / (1.0 + jnp.exp(-x))


def _ffn_kernel(x_ref, g_ref, wg_ref, wu_ref, wd_ref, o_ref, h_scr, acc_scr):
    f = pl.program_id(1)

    @pl.when(f == 0)
    def _():
        h_scr[...] = _rms(x_ref[...], g_ref[...]).astype(BF16)
        acc_scr[...] = jnp.zeros(acc_scr.shape, F32)

    h = h_scr[...]
    act = _silu(_dot(h, wg_ref[...])) * _dot(h, wu_ref[...])
    acc_scr[...] += _dot(act.astype(BF16), wd_ref[...])

    @pl.when(f == pl.num_programs(1) - 1)
    def _():
        o_ref[...] = x_ref[...] + acc_scr[...]


def ffn_dense(x, g, wg, wu, wd, *, tm, tf):
    m, d = x.shape
    f = wg.shape[1]
    return pl.pallas_call(
        _ffn_kernel,
        out_shape=jax.ShapeDtypeStruct((m, d), F32),
        grid=(m // tm, f // tf),
        in_specs=[
            pl.BlockSpec((tm, d), lambda i, j: (i, 0)),
            pl.BlockSpec((1, d), lambda i, j: (0, 0)),
            pl.BlockSpec((d, tf), lambda i, j: (0, j)),
            pl.BlockSpec((d, tf), lambda i, j: (0, j)),
            pl.BlockSpec((tf, d), lambda i, j: (j, 0)),
        ],
        out_specs=pl.BlockSpec((tm, d), lambda i, j: (i, 0)),
        scratch_shapes=[pltpu.VMEM((tm, d), BF16), pltpu.VMEM((tm, d), F32)],
        compiler_params=_cparams(2),
        name="ffn_dense",
    )(x, g.reshape(1, d), wg, wu, wd)


def _router_kernel(x_ref, g_ref, wr_ref, br_ref, h_ref, idx_ref, gate_ref):
    h = _rms(x_ref[...], g_ref[...]).astype(BF16)
    h_ref[...] = h
    logits = _dot(h, wr_ref[...]) + br_ref[...]
    lane_i = lax.broadcasted_iota(jnp.int32, logits.shape, 1)
    lane = lane_i.astype(F32)
    big = float(ROUTER_LANES)
    t1 = jnp.max(logits, axis=-1, keepdims=True)
    i1 = jnp.min(jnp.where(logits == t1, lane, big), axis=-1, keepdims=True)
    rest = jnp.where(lane == i1, NEG_INF, logits)
    t2 = jnp.max(rest, axis=-1, keepdims=True)
    i2 = jnp.min(jnp.where(rest == t2, lane, big), axis=-1, keepdims=True)
    e2 = jnp.exp(t2 - t1)
    g1 = 1.0 / (1.0 + e2)
    idx_ref[...] = jnp.where(lane_i == 0, i1, i2).astype(jnp.int32)
    gate_ref[...] = jnp.where(lane_i == 0, g1, e2 * g1)


def moe_router(x, g, w_router_pad, b_router_pad, *, tm):
    m, d = x.shape
    wide = pl.BlockSpec((tm, ROUTER_LANES), lambda i: (i, 0))
    return pl.pallas_call(
        _router_kernel,
        out_shape=(jax.ShapeDtypeStruct((m, d), BF16), jax.ShapeDtypeStruct((m, ROUTER_LANES), jnp.int32),
                   jax.ShapeDtypeStruct((m, ROUTER_LANES), F32)),
        grid=(m // tm,),
        in_specs=[
            pl.BlockSpec((tm, d), lambda i: (i, 0)),
            pl.BlockSpec((1, d), lambda i: (0, 0)),
            pl.BlockSpec((d, ROUTER_LANES), lambda i: (0, 0)),
            pl.BlockSpec((1, ROUTER_LANES), lambda i: (0, 0)),
        ],
        out_specs=(pl.BlockSpec((tm, d), lambda i: (i, 0)), wide, wide),
        compiler_params=_cparams(1),
        name="moe_router",
    )(x, g.reshape(1, d), w_router_pad, b_router_pad)


def _moe_ffn_kernel(be_ref, nused_ref, x_ref, wg_ref, wu_ref, wd_ref, o_ref, acc_scr):
    b = pl.program_id(0)
    f = pl.program_id(1)

    @pl.when(b < nused_ref[0])
    def _():
        @pl.when(f == 0)
        def _():
            acc_scr[...] = jnp.zeros(acc_scr.shape, F32)

        x = x_ref[...]
        act = _silu(_dot(x, wg_ref[...])) * _dot(x, wu_ref[...])
        acc_scr[...] += _dot(act.astype(BF16), wd_ref[...])

        @pl.when(f == pl.num_programs(1) - 1)
        def _():
            o_ref[...] = acc_scr[...].astype(o_ref.dtype)

    @pl.when(jnp.logical_and(b >= nused_ref[0], f == pl.num_programs(1) - 1))
    def _():
        o_ref[...] = jnp.zeros(o_ref.shape, o_ref.dtype)


def moe_ffn_grouped(block_expert, n_used, x_buf, wg, wu, wd, *, bm, tf, out_dtype):
    l, d = x_buf.shape
    n_f = wg.shape[2] // tf

    def fidx(b, f, nused):
        return jnp.where(b < nused[0], f, n_f - 1)

    grid_spec = pltpu.PrefetchScalarGridSpec(
        num_scalar_prefetch=2,
        grid=(l // bm, n_f),
        in_specs=[
            pl.BlockSpec((bm, d), lambda b, f, be, nu: (b, 0)),
            pl.BlockSpec((None, d, tf), lambda b, f, be, nu: (be[b], 0, fidx(b, f, nu))),
            pl.BlockSpec((None, d, tf), lambda b, f, be, nu: (be[b], 0, fidx(b, f, nu))),
            pl.BlockSpec((None, tf, d), lambda b, f, be, nu: (be[b], fidx(b, f, nu), 0)),
        ],
        out_specs=pl.BlockSpec((bm, d), lambda b, f, be, nu: (b, 0)),
        scratch_shapes=[pltpu.VMEM((bm, d), F32)],
    )
    return pl.pallas_call(
        _moe_ffn_kernel,
        out_shape=jax.ShapeDtypeStruct((l, d), out_dtype),
        grid_spec=grid_spec,
        compiler_params=_cparams(2),
        name="moe_ffn_grouped",
    )(block_expert, n_used, x_buf, wg, wu, wd)


def _combine_norm_kernel(x_ref, y0_ref, y1_ref, gate_ref, g_ref, xo_ref, yo_ref):
    gate = gate_ref[...]
    x = x_ref[...] + gate[:, 0:1] * y0_ref[...].astype(F32) + gate[:, 1:2] * y1_ref[...].astype(F32)
    xo_ref[...] = x
    yo_ref[...] = _rms(x, g_ref[...])


def combine_norm(x, y0, y1, gates, g_final, *, tm):
    m, d = x.shape
    blk = pl.BlockSpec((tm, d), lambda i: (i, 0))
    return pl.pallas_call(
        _combine_norm_kernel,
        out_shape=(jax.ShapeDtypeStruct((m, d), F32), jax.ShapeDtypeStruct((m, d), F32)),
        grid=(m // tm,),
        in_specs=[blk, blk, blk, pl.BlockSpec((tm, ROUTER_LANES), lambda i: (i, 0)),
                  pl.BlockSpec((1, d), lambda i: (0, 0))],
        out_specs=(blk, blk),
        compiler_params=_cparams(1),
        name="combine_norm",
    )(x, y0, y1, gates, g_final.reshape(1, d))


def moe_layer(x, g, w_router, b_router, wg, wu, wd, g_final, *, tm, bm, tf, y_dtype):
    m, d = x.shape
    wr = jnp.zeros((d, ROUTER_LANES), BF16).at[:, :N_EXPERTS].set(w_router.astype(BF16))
    br = jnp.full((1, ROUTER_LANES), NEG_INF, F32).at[0, :N_EXPERTS].set(b_router)
    h, idx, gates = moe_router(x, g, wr, br, tm=tm)

    nk = m * TOP_K
    flat_e = idx[:, :TOP_K].reshape(nk)
    onehot = (flat_e[:, None] == jnp.arange(N_EXPERTS, dtype=jnp.int32)[None, :]).astype(jnp.int32)
    csum = jnp.cumsum(onehot, axis=0)
    counts = csum[-1]
    rank = jnp.take_along_axis(csum, flat_e[:, None], axis=1)[:, 0] - 1
    padded = (counts + bm - 1) // bm * bm
    pad_end = jnp.cumsum(padded)
    pad_start = pad_end - padded
    dest = pad_start[flat_e] + rank
    l_rows = (nk + bm - 1) // bm * bm + N_EXPERTS * bm
    n_blocks = l_rows // bm
    tok_buf = jnp.zeros((l_rows,), jnp.int32).at[dest].set(jnp.arange(nk, dtype=jnp.int32) // TOP_K)
    block_expert = jnp.clip(
        jnp.searchsorted(pad_end, jnp.arange(n_blocks, dtype=jnp.int32) * bm, side="right"), 0, N_EXPERTS - 1
    ).astype(jnp.int32)
    n_used = (pad_end[-1] // bm).astype(jnp.int32).reshape(1)

    x_buf = jnp.take(h, tok_buf, axis=0)
    y_buf = moe_ffn_grouped(block_expert, n_used, x_buf, wg, wu, wd, bm=bm, tf=tf, out_dtype=y_dtype)
    dest2 = dest.reshape(m, TOP_K)
    y0 = jnp.take(y_buf, dest2[:, 0], axis=0)
    y1 = jnp.take(y_buf, dest2[:, 1], axis=0)
    return combine_norm(x, y0, y1, gates, g_final, tm=tm)


def _sample_mix_weights(sgu_w, sgu_b, t_new, n_seq):
    r = jnp.arange(n_seq * t_new)
    t = r % t_new
    same = (r[:, None] // t_new) == (r[None, :] // t_new)
    causal = t[None, :] <= t[:, None]
    w = sgu_w[:, t[:, None], t[None, :]] * (same & causal)[None].astype(sgu_w.dtype)
    bias = jnp.repeat(sgu_b[:, t].T, CHUNK, axis=1)
    return w, bias


def kernel(x_prompt, x_sample, mem_prompt, cache_attn_k, cache_attn_v, cache_mem_k, cache_mem_v, page_table, norm_mix, w_in, b_gate, sgu_ln_g, sgu_ln_b, sgu_w, sgu_b, lam_q1, lam_k1, lam_q2, lam_k2, subln_g, mem_norm, w_mem_k, w_mem_v, w_br_a, w_br_b, w_br_m, w_out, norm_ffn, w_d_gate, w_d_up, w_d_down, w_router, b_router, w_e_gate, w_e_up, w_e_down, norm_final):
    batch, seq, d = x_prompt.shape
    n_seq, t_new, _ = x_sample.shape
    depth = w_in.shape[0]
    n_pages = page_table.shape[1]
    past_len = n_pages * PAGE_SIZE
    mp = batch * seq
    ms = n_seq * t_new
    assert ms == CHUNK and d == D_MODEL and N_QROWS == 2 * DA_HEADS * t_new

    xp = x_prompt.reshape(mp, d)
    xs = x_sample.reshape(ms, d)
    mem = mem_prompt.reshape(batch * N_MEM, d)
    ck = cache_attn_k.reshape(cache_attn_k.shape[0], cache_attn_k.shape[1], PAGE_SIZE * DA_HEADS, DA_V_DIM)
    cv = cache_attn_v.reshape(cache_attn_v.shape[0], cache_attn_v.shape[1], PAGE_SIZE * DA_HEADS, DA_V_DIM)
    slopes =jnp.asarray([2.0 ** (-8.0 * (i + 1) / DA_HEADS) for i in range(DA_HEADS)], F32)
    row_slopes = jnp.repeat(slopes, 2 * t_new).reshape(N_QROWS, 1)
    tril = jnp.tril(jnp.ones((CHUNK, CHUNK), F32))

    c_q = 2 * SGU_WIDTH
    c_k = c_q + DA_WIDTH
    c_qm = c_k + 2 * DA_WIDTH
    c_gate = c_qm + MEM_WIDTH

    outs = {k: [] for k in ("pk", "pv", "pmk", "pmv", "sk", "sv", "scv")}
    yp = ys = None
    for l in range(depth):
        lam_init = 0.8 - 0.6 * math.exp(-0.3 * l)
        lamp = jnp.stack([lam_q1[l], lam_k1[l], lam_q2[l], lam_k2[l]])
        w_l = w_in[l]
        w_z = jnp.concatenate([w_l[:, :c_k], w_l[:, c_qm:c_gate]], axis=1).astype(BF16)
        w_kv = w_l[:, c_k:c_qm].astype(BF16)
        w_g = w_l[:, c_gate:].astype(BF16)
        zero_bias = jnp.zeros((w_z.shape[1],), F32)
        w_mem = jnp.concatenate([w_mem_k[l], w_mem_v[l]], axis=1).astype(BF16)
        wa, wb, wm, wo = (w.astype(BF16) for w in (w_br_a[l], w_br_b[l], w_br_m[l], w_out[l]))
        w_mix_p = (sgu_w[l] * tril[None]).astype(BF16)
        bias_p = jnp.repeat(sgu_b[l].T, CHUNK, axis=1)
        w_mix_s, bias_s = _sample_mix_weights(sgu_w[l], sgu_b[l], t_new, n_seq)
        qm_blk = c_k // MEM_WIDTH

        mkv = norm_matmul(mem, mem_norm[l], w_mem, jnp.zeros((2 * MEM_WIDTH,), F32), sigmoid=False,
                          out_dtype=F32, tm=batch * N_MEM, tn=MEM_WIDTH)
        mk_p = mkv[:, :MEM_WIDTH].reshape(batch, N_MEM, MEM_WIDTH)
        mv_p = mkv[:, MEM_WIDTH:].reshape(batch, N_MEM, MEM_WIDTH)

        z = norm_matmul(xp, norm_mix[l], w_z, zero_bias, sigmoid=False, out_dtype=BF16, tm=1024, tn=512)
        kf, vf, kb, vt = norm_kv(xp, norm_mix[l], w_kv, tm=512, batch=batch)
        gates = norm_matmul(xp, norm_mix[l], w_g, b_gate[l], sigmoid=True, out_dtype=BF16, tm=1024, tn=512)
        (a,) = sgu(z, sgu_ln_g[l], sgu_ln_b[l], w_mix_p, bias_p, n_chunks=4, emit_v=False)
        bo = diff_attn_prompt(z, kb, vt, slopes, lamp, subln_g[l], batch=batch, seq=seq, lam_init=lam_init,
                              q_col0=c_q // DA_V_DIM, tq=1024)
        mo = mem_attn(z, mk_p, mv_p, n_batch=batch, rows_per_batch=seq, tq=512, q_col0=qm_blk)
        xp = merge(xp, a, bo, mo, gates, wa, wb, wm, wo, tm=512)

        zs = norm_matmul(xs, norm_mix[l], w_z, zero_bias, sigmoid=False, out_dtype=F32, tm=ms, tn=512)
        kfs, vfs, _, _ = norm_kv(xs, norm_mix[l], w_kv, tm=ms, batch=1)
        gates_s = norm_matmul(xs, norm_mix[l], w_g, b_gate[l], sigmoid=True, out_dtype=F32, tm=ms, tn=512)
        a_s, vrows = sgu(zs, sgu_ln_g[l], sgu_ln_b[l], w_mix_s.astype(BF16), bias_s, n_chunks=1, emit_v=True)
        q_s = zs[:, c_q:c_k].reshape(n_seq, 1, t_new, DA_WIDTH)
        q_rep = jnp.broadcast_to(q_s, (n_seq, 2 * DA_HEADS, t_new, DA_WIDTH)).reshape(n_seq, N_QROWS, DA_WIDTH)
        bo_s = diff_attn_sample(page_table, q_rep, kfs.reshape(n_seq, t_new, DA_WIDTH),
                                vfs.reshape(n_seq, t_new, DA_WIDTH), ck, cv, lamp, row_slopes, subln_g[l], layer=l,
                                past_len=past_len, lam_init=lam_init, pages_per_step=8)
        qm_s = zs[:, c_k:c_k + MEM_WIDTH].reshape(n_seq, t_new, MEM_WIDTH)
        qm_s = jnp.pad(qm_s, ((0, 0), (0, QM_PAD_ROWS - t_new), (0, 0))).reshape(n_seq * QM_PAD_ROWS, MEM_WIDTH)
        mo_s = mem_attn(qm_s, cache_mem_k[l].reshape(n_seq, N_MEM, MEM_WIDTH),
                        cache_mem_v[l].reshape(n_seq, N_MEM, MEM_WIDTH), n_batch=n_seq,
                        rows_per_batch=QM_PAD_ROWS, tq=QM_PAD_ROWS, q_col0=0)
        mo_s = mo_s.reshape(n_seq, QM_PAD_ROWS, MEM_WIDTH)[:, :t_new].reshape(ms, MEM_WIDTH)
        xs = merge(xs, a_s, bo_s.reshape(ms, DA_WIDTH), mo_s, gates_s, wa, wb, wm, wo, tm=ms)

        if l % 2 == 0:
            i = l // 2
            wg, wu, wd = w_d_gate[i].astype(BF16), w_d_up[i].astype(BF16), w_d_down[i].astype(BF16)
            xp = ffn_dense(xp, norm_ffn[l], wg, wu, wd, tm=512, tf=1408)
            xs = ffn_dense(xs, norm_ffn[l], wg, wu, wd, tm=ms, tf=1408)
        else:
            i = l // 2
            wg, wu, wd = w_e_gate[i].astype(BF16), w_e_up[i].astype(BF16), w_e_down[i].astype(BF16)
            xp, yp = moe_layer(xp, norm_ffn[l], w_router[i], b_router[i], wg, wu, wd, norm_final, tm=512, bm=512,
                               tf=512, y_dtype=BF16)
            xs, ys = moe_layer(xs, norm_ffn[l], w_router[i], b_router[i], wg, wu, wd, norm_final, tm=ms, bm=128,
                               tf=512, y_dtype=F32)

        outs["pk"].append(kf.reshape(batch, seq, DA_HEADS, DA_V_DIM))
        outs["pv"].append(vf.reshape(batch, seq, DA_HEADS, DA_V_DIM))
        outs["pmk"].append(mk_p.reshape(batch, N_MEM, MEM_HEADS, MEM_HEAD_DIM))
        outs["pmv"].append(mv_p.reshape(batch, N_MEM, MEM_HEADS, MEM_HEAD_DIM))
        outs["sk"].append(kfs.reshape(n_seq, t_new, DA_HEADS, DA_V_DIM))
        outs["sv"].append(vfs.reshape(n_seq, t_new, DA_HEADS, DA_V_DIM))
        outs["scv"].append(vrows.reshape(n_seq, t_new, SGU_WIDTH))

    if depth % 2 == 1:
        raise NotImplementedError("final norm is fused into the expert layer, which must come last")
    return (yp.reshape(batch, seq, d), ys.reshape(n_seq, t_new, d),
            jnp.stack(outs["pk"]), jnp.stack(outs["pv"]), jnp.stack(outs["pmk"]), jnp.stack(outs["pmv"]),
            jnp.stack(outs["sk"]), jnp.stack(outs["sv"]), jnp.stack(outs["scv"]))
```

```python
import functools
import math

import jax
import jax.numpy as jnp
from jax import lax
from jax.experimental import pallas as pl
from jax.experimental.pallas import tpu as pltpu

F32 = jnp.float32
BF16 = jnp.bfloat16

D_MODEL = 1024
CHUNK = 128
SGU_GROUPS = 4
SGU_WIDTH = 512
DA_HEADS = 8
DA_QK_DIM = 64
DA_V_DIM = 128
DA_WIDTH = DA_HEADS * DA_V_DIM
MEM_HEADS = 4
MEM_HEAD_DIM = 128
MEM_WIDTH = MEM_HEADS * MEM_HEAD_DIM
N_MEM = 256
N_EXPERTS = 8
TOP_K = 2
PAGE_SIZE = 128
EPS = 1e-5
NEG_INF = float("-inf")

VMEM_LIMIT_BYTES = 48 * 1024 * 1024
LANES = 128
ROUTER_LANES = 128
QM_PAD_ROWS = 16
ALIBI_SPLIT = 16
SUM_ROWS = 16
LOG2E = 1.4426950408889634


def _cparams(n_axes):
    return pltpu.CompilerParams(
        dimension_semantics=("arbitrary",) * n_axes, vmem_limit_bytes=VMEM_LIMIT_BYTES)


def _rms(x, g):
    ms = jnp.mean(x * x, axis=-1, keepdims=True)
    return x * lax.rsqrt(ms + EPS) * g


def _gelu(x):
    return 0.5 * x * (1.0 + lax.erf(x * (1.0 / math.sqrt(2.0))))


def _dot(a, b):
    return jnp.dot(a, b, preferred_element_type=F32)


def _dot_nt(a, b):
    return lax.dot_general(a, b, (((1,), (1,)), ((), ())), preferred_element_type=F32)


def _norm_matmul_kernel(x_ref, g_ref, w_ref, b_ref, o_ref, h_scr, *, sigmoid):
    @pl.when(pl.program_id(1) == 0)
    def _():
        h_scr[...] = _rms(x_ref[...], g_ref[...]).astype(BF16)

    acc = _dot(h_scr[...], w_ref[...])
    if sigmoid:
        acc = 1.0 / (1.0 + jnp.exp(-(acc + b_ref[...])))
    o_ref[...] = acc.astype(o_ref.dtype)


def norm_matmul(x, g, w, bias, *, sigmoid, out_dtype, tm, tn):
    m, d = x.shape
    n = w.shape[1]
    return pl.pallas_call(
        functools.partial(_norm_matmul_kernel, sigmoid=sigmoid),
        out_shape=jax.ShapeDtypeStruct((m, n), out_dtype),
        grid=(m // tm, n // tn),
        in_specs=[
            pl.BlockSpec((tm, d), lambda i, j: (i, 0)),
            pl.BlockSpec((1, d), lambda i, j: (0, 0)),
            pl.BlockSpec((d, tn), lambda i, j: (0, j)),
            pl.BlockSpec((1, tn), lambda i, j: (0, j)),
        ],
        out_specs=pl.BlockSpec((tm, tn), lambda i, j: (i, j)),
        scratch_shapes=[pltpu.VMEM((tm, d), BF16)],
        compiler_params=_cparams(2),
        name="norm_matmul_sig" if sigmoid else "norm_matmul",
    )(x, g.reshape(1, d), w, bias.reshape(1, n))


def _norm_kv_kernel(x_ref, g_ref, w_ref, kf_ref, vf_ref, kb_ref, vt_ref, h_scr):
    j = pl.program_id(1)

    @pl.when(j == 0)
    def _():
        h_scr[...] = _rms(x_ref[...], g_ref[...]).astype(BF16)

    acc = _dot(h_scr[...], w_ref[...])

    @pl.when(j == 0)
    def _():
        kf_ref[...] = acc
        kb_ref[...] = acc.astype(BF16)

    @pl.when(j == 1)
    def _():
        vf_ref[...] = acc
        vt_ref[...] = acc.T.astype(BF16)


def norm_kv(x, g, w_kv, *, tm, batch):
    m, d = x.shape
    n = w_kv.shape[1] // 2
    nb = m // batch // tm
    blk = pl.BlockSpec((tm, n), lambda i, j: (i, 0))
    return pl.pallas_call(
        _norm_kv_kernel,
        out_shape=(jax.ShapeDtypeStruct((m, n), F32), jax.ShapeDtypeStruct((m, n), F32),
                   jax.ShapeDtypeStruct((m, n), BF16), jax.ShapeDtypeStruct((batch, n, m // batch), BF16)),
        grid=(m // tm, 2),
        in_specs=[
            pl.BlockSpec((tm, d), lambda i, j: (i, 0)),
            pl.BlockSpec((1, d), lambda i, j: (0, 0)),
            pl.BlockSpec((d, n), lambda i, j: (0, j)),
        ],
        out_specs=(blk, blk, blk, pl.BlockSpec((None, n, tm), lambda i, j: (i // nb, 0, i % nb))),
        scratch_shapes=[pltpu.VMEM((tm, d), BF16)],
        compiler_params=_cparams(2),
        name="norm_kv",
    )(x, g.reshape(1, d), w_kv)


def _sgu_kernel(u_ref, v_ref, lng_ref, lnb_ref, w_ref, bias_ref, a_ref, *maybe_vn_ref, n_chunks):
    u = _gelu(u_ref[...].astype(F32))
    v = _gelu(v_ref[...].astype(F32))
    mu = jnp.mean(v, axis=-1, keepdims=True)
    vc = v - mu
    var = jnp.mean(vc * vc, axis=-1, keepdims=True)
    vn = vc * lax.rsqrt(var + EPS) * lng_ref[...] + lnb_ref[...]
    if maybe_vn_ref:
        maybe_vn_ref[0][...] = vn
    vnb = vn.astype(BF16)
    bias = bias_ref[...]
    for c in range(n_chunks):
        rows = slice(c * CHUNK, (c + 1) * CHUNK)
        for g in range(SGU_GROUPS):
            cols = slice(g * CHUNK, (g + 1) * CHUNK)
            mixed = _dot(w_ref[g], vnb[rows, cols]) + bias[:, cols]
            a_ref[rows, cols] = (u[rows, cols] * mixed).astype(a_ref.dtype)


def sgu(z, ln_g, ln_b, w_mix, bias_full, *, n_chunks, emit_v):
    m = z.shape[0]
    rows = n_chunks * CHUNK
    row_blk = pl.BlockSpec((rows, SGU_WIDTH), lambda i: (i, 0))
    out_shape = [jax.ShapeDtypeStruct((m, SGU_WIDTH), BF16)]
    out_specs = [row_blk]
    if emit_v:
        out_shape.append(jax.ShapeDtypeStruct((m, SGU_WIDTH), F32))
        out_specs.append(row_blk)
    return pl.pallas_call(
        functools.partial(_sgu_kernel, n_chunks=n_chunks),
        out_shape=tuple(out_shape),
        grid=(m // rows,),
        in_specs=[
            row_blk,
            pl.BlockSpec((rows, SGU_WIDTH), lambda i: (i, 1)),
            pl.BlockSpec((1, SGU_WIDTH), lambda i: (0, 0)),
            pl.BlockSpec((1, SGU_WIDTH), lambda i: (0, 0)),
            pl.BlockSpec((SGU_GROUPS, CHUNK, CHUNK), lambda i: (0, 0, 0)),
            pl.BlockSpec((CHUNK, SGU_WIDTH), lambda i: (0, 0)),
        ],
        out_specs=tuple(out_specs),
        compiler_params=_cparams(1),
        name="sgu",
    )(z, z, ln_g.reshape(1, -1), ln_b.reshape(1, -1), w_mix, bias_full)


def _diff_lambda(lamp, lam_init):
    s1 = jnp.sum(lamp[0:1, :] * lamp[1:2, :], axis=-1, keepdims=True)
    s2 = jnp.sum(lamp[2:3, :] * lamp[3:4, :], axis=-1, keepdims=True)
    return jnp.exp(s1) - jnp.exp(s2) + lam_init


def _sub_norm(o, g, lam_init):
    return _rms(o, g) * (1.0 - lam_init)


def _diff_attn_kernel(slopes_ref, lamp_ref, q_ref, k_ref, vt_ref, g_ref, o_ref, causal_scr, *, tq, lam_init):
    h = pl.program_id(1)
    qi = pl.program_id(2)
    slope2 = slopes_ref[h] * LOG2E

    @pl.when((pl.program_id(0) == 0) & (h == 0) & (qi == 0))
    def _():
        krow = lax.broadcasted_iota(jnp.int32, (tq, tq), 0)
        qcol = lax.broadcasted_iota(jnp.int32, (tq, tq), 1)
        causal_scr[...] = jnp.where(krow <= qcol, 0.0, NEG_INF)

    q = (q_ref[...].astype(F32) * (DA_QK_DIM ** -0.5 * LOG2E)).astype(BF16)
    lane = lax.broadcasted_iota(jnp.int32, (tq, LANES), 1)
    pos = lax.broadcasted_iota(jnp.int32, (tq, LANES), 0)
    hi = (pos // ALIBI_SPLIT).astype(F32)
    lo = (pos % ALIBI_SPLIT).astype(F32)
    c_hi = slope2.astype(BF16).astype(F32)
    c_lo = (slope2 - c_hi).astype(BF16).astype(F32)
    q_alibi = jnp.zeros((tq, LANES), F32)
    k_alibi = jnp.zeros((tq, LANES), F32)
    for n, c in enumerate((c_hi, c_lo)):
        q_alibi = jnp.where(lane == 4 * n, c * ALIBI_SPLIT, jnp.where(lane == 4 * n + 1, c, jnp.where(
            lane == 4 * n + 2, -c * ALIBI_SPLIT * hi, jnp.where(lane == 4 * n + 3, -c * lo, q_alibi))))
        k_alibi = jnp.where(lane == 4 * n, hi, jnp.where(lane == 4 * n + 1, lo, jnp.where(
            (lane == 4 * n + 2) | (lane == 4 * n + 3), 1.0, k_alibi)))
    q_alibi = q_alibi.astype(BF16)
    k_alibi = k_alibi.astype(BF16)
    qzero = jnp.zeros_like(q)
    qz = (jnp.concatenate([jnp.where(lane < DA_QK_DIM, q, qzero), q_alibi], axis=1),
          jnp.concatenate([jnp.where(lane >= DA_QK_DIM, q, qzero), q_alibi], axis=1))
    ones = jnp.ones((SUM_ROWS, tq), BF16)

    def update(carry, start, mask, shift):
        kt = jnp.concatenate([k_ref[pl.ds(start, tq), :], k_alibi], axis=1)
        vt = jnp.concatenate([vt_ref[:, pl.ds(start, tq)], ones], axis=0)
        ts = [_dot_nt(kt, qm) for qm in qz]
        new = []
        for (m, acc), t in zip(carry, ts):
            if mask is not None:
                t = t + mask
            m_new = jnp.maximum(m, jnp.max(t, axis=0, keepdims=True) + shift)
            p = jnp.exp2(t - (m_new - shift))
            acc_new = jnp.exp2(m - m_new) * acc + _dot(vt, p.astype(BF16))
            new.append((m_new, acc_new))
        return tuple(new)

    def body(j, carry):
        shift = ((j - qi) * tq).astype(F32) * slope2
        return update(carry, pl.multiple_of(j * tq, tq), None, shift)

    one = (jnp.full((1, tq), NEG_INF, F32), jnp.zeros((DA_V_DIM + SUM_ROWS, tq), F32))
    carry = lax.fori_loop(0, qi, body, (one, one))
    (_, a1), (_, a2) = update(carry, pl.multiple_of(qi * tq, tq), causal_scr[...], 0.0)
    lam = _diff_lambda(lamp_ref[...], lam_init)
    o1 = a1[:DA_V_DIM] / a1[DA_V_DIM:DA_V_DIM + 1]
    o2 = a2[:DA_V_DIM] / a2[DA_V_DIM:DA_V_DIM + 1]
    o = (o1 - lam * o2).T
    o_ref[...] = _sub_norm(o, g_ref[...], lam_init).astype(o_ref.dtype)


def diff_attn_prompt(z, kb, vt, slopes, lamp, subln_g, *, batch, seq, lam_init, q_col0, tq):
    nq = seq // tq
    grid_spec = pltpu.PrefetchScalarGridSpec(
        num_scalar_prefetch=1,
        grid=(batch, DA_HEADS, nq),
        in_specs=[
            pl.BlockSpec((4, DA_QK_DIM), lambda b, h, i, s: (0, 0)),
            pl.BlockSpec((tq, DA_V_DIM), lambda b, h, i, s: (b * nq + i, q_col0 + h)),
            pl.BlockSpec((seq, DA_V_DIM), lambda b, h, i, s: (b, h)),
            pl.BlockSpec((None, DA_V_DIM, seq), lambda b, h, i, s: (b, h, 0)),
            pl.BlockSpec((1, DA_V_DIM), lambda b, h, i, s: (0, 0)),
        ],
        out_specs=pl.BlockSpec((tq, DA_V_DIM), lambda b, h, i, s: (b * nq + i, h)),
        scratch_shapes=[pltpu.VMEM((tq, tq), F32)],
    )
    return pl.pallas_call(
        functools.partial(_diff_attn_kernel, tq=tq, lam_init=lam_init),
        out_shape=jax.ShapeDtypeStruct((batch * seq, DA_WIDTH), BF16),
        grid_spec=grid_spec,
        compiler_params=_cparams(3),
        name="diff_attn_prompt",
    )(slopes, lamp, z, kb, vt, subln_g.reshape(1, -1))


N_QROWS = DA_HEADS * 2 * 4


def _paged_attn_kernel(pt_ref, lamp_ref, slope_ref, q_ref, kn_ref, vn_ref, g_ref, *rest, pages_per_step, t_new,
                       past_len, lam_init):
    pp = pages_per_step
    k_refs = rest[:pp]
    v_refs = rest[pp:2 * pp]
    o_ref = rest[2 * pp]
    qz_scr, s_scr, sn_scr, m_scr, l_scr, acc_scr = rest[2 * pp + 1:]
    phase = pl.program_id(1)
    g = pl.program_id(2)
    last = pl.num_programs(2) - 1
    step_keys = pp * PAGE_SIZE
    rows_per_head = 2 * t_new

    rowi = lax.broadcasted_iota(jnp.int32, (N_QROWS, 1), 0)
    slope = slope_ref[...]
    tok = rowi % t_new
    lam = _diff_lambda(lamp_ref[...], lam_init)

    def page(ref):
        return jnp.concatenate(
            [ref[pl.ds(h, PAGE_SIZE, stride=DA_HEADS), :].astype(BF16) for h in range(DA_HEADS)], axis=1)

    def combine(p):
        return pltpu.roll(p, t_new, axis=0) - lam * p

    def new_key_scores():
        kn = jnp.concatenate([kn_ref[...].astype(BF16), jnp.zeros((PAGE_SIZE - t_new, DA_WIDTH), BF16)], axis=0)
        s_all = _dot_nt(qz_scr[...], kn)
        row = lax.broadcasted_iota(jnp.int32, s_all.shape, 0)
        j = lax.broadcasted_iota(jnp.int32, s_all.shape, 1)
        valid = (j < t_new) & (j <= row % t_new)
        return jnp.where(valid, s_all - slope * (row % t_new - j).astype(F32), NEG_INF)

    @pl.when(jnp.logical_and(phase == 0, g == 0))
    def _():
        row = lax.broadcasted_iota(jnp.int32, (N_QROWS, DA_WIDTH), 0)
        lane = lax.broadcasted_iota(jnp.int32, (N_QROWS, DA_WIDTH), 1)
        keep = (lane // DA_QK_DIM) == (row // t_new)
        qz_scr[...] = jnp.where(keep, q_ref[...] * (DA_QK_DIM ** -0.5), 0.0).astype(BF16)
        m_scr[...] = jnp.full(m_scr.shape, NEG_INF, F32)
        l_scr[...] = jnp.zeros(l_scr.shape, F32)
        acc_scr[...] = jnp.zeros(acc_scr.shape, F32)

    def online(s):
        m_old = m_scr[...]
        m_new = jnp.maximum(m_old, jnp.max(s, axis=-1, keepdims=True))
        l_scr[...] = jnp.exp(m_old - m_new) * l_scr[...] + jnp.sum(jnp.exp(s - m_new), axis=-1, keepdims=True)
        m_scr[...] = m_new

    @pl.when(phase == 0)
    def _():
        qz = qz_scr[...]
        s = jnp.concatenate([_dot_nt(qz, page(k_refs[r])) for r in range(pp)], axis=1)
        kpos = g * step_keys + lax.broadcasted_iota(jnp.int32, (1, step_keys), 1)
        s = s - slope * ((past_len + tok) - kpos).astype(F32)
        s_scr[:, pl.ds(pl.multiple_of(g * step_keys, step_keys), step_keys)] = s
        online(s)

        @pl.when(g == last)
        def _():
            sn = new_key_scores()
            sn_scr[...] = sn
            online(sn)

    @pl.when(phase == 1)
    def _():
        s = s_scr[:, pl.ds(pl.multiple_of(g * step_keys, step_keys), step_keys)]
        m = m_scr[...]
        inv_l = 1.0 / l_scr[...]
        pc = combine(jnp.exp(s - m) * inv_l).astype(BF16)
        pv = _dot(pc[:, 0:PAGE_SIZE], page(v_refs[0]))
        for r in range(1, pp):
            pv += _dot(pc[:, r * PAGE_SIZE:(r + 1) * PAGE_SIZE], page(v_refs[r]))
        acc_scr[...] += pv

        @pl.when(g == last)
        def _():
            pn = combine(jnp.exp(sn_scr[...] - m) * inv_l).astype(BF16)
            vn = jnp.concatenate([vn_ref[...].astype(BF16), jnp.zeros((PAGE_SIZE - t_new, DA_WIDTH), BF16)], axis=0)
            acc = acc_scr[...] + _dot(pn, vn)
            outs = []
            for h in range(DA_HEADS):
                r0 = h * rows_per_head + t_new
                outs.append(_sub_norm(acc[r0:r0 + t_new, h * DA_V_DIM:(h + 1) * DA_V_DIM], g_ref[...], lam_init))
            o_ref[...] = jnp.concatenate(outs, axis=1)


def diff_attn_sample(page_table, q_rep, k_new, v_new, cache_k, cache_v, lamp, row_slopes, subln_g, *, layer,
                     past_len, lam_init, pages_per_step):
    n_seq, t_new = k_new.shape[:2]
    n_pages = page_table.shape[1]
    pp = pages_per_step
    n_groups = n_pages // pp
    page_blk = (None, None, PAGE_SIZE * DA_HEADS, DA_V_DIM)

    def k_spec(r):
        return pl.BlockSpec(
            page_blk, lambda b, ph, g, pt: (layer, pt[b, jnp.where(ph == 0, g, n_groups - 1) * pp + r], 0, 0))

    def v_spec(r):
        return pl.BlockSpec(page_blk, lambda b, ph, g, pt: (layer, pt[b, jnp.where(ph == 0, 0, g) * pp + r], 0, 0))

    new_spec = pl.BlockSpec((None, t_new, DA_WIDTH), lambda b, ph, g, pt: (b, 0, 0))
    grid_spec = pltpu.PrefetchScalarGridSpec(
        num_scalar_prefetch=1,
        grid=(n_seq, 2, n_groups),
        in_specs=[
            pl.BlockSpec((4, DA_QK_DIM), lambda b, ph, g, pt: (0, 0)),
            pl.BlockSpec((N_QROWS, 1), lambda b, ph, g, pt: (0, 0)),
            pl.BlockSpec((None, N_QROWS, DA_WIDTH), lambda b, ph, g, pt: (b, 0, 0)),
            new_spec, new_spec,
            pl.BlockSpec((1, DA_V_DIM), lambda b, ph, g, pt: (0, 0)),
        ] + [k_spec(r) for r in range(pp)] + [v_spec(r) for r in range(pp)],
        out_specs=pl.BlockSpec((None, t_new, DA_WIDTH), lambda b, ph, g, pt: (b, 0, 0)),
        scratch_shapes=[
            pltpu.VMEM((N_QROWS, DA_WIDTH), BF16),
            pltpu.VMEM((N_QROWS, n_pages * PAGE_SIZE), F32),
            pltpu.VMEM((N_QROWS, PAGE_SIZE), F32),
            pltpu.VMEM((N_QROWS, 1), F32),
            pltpu.VMEM((N_QROWS, 1), F32),
            pltpu.VMEM((N_QROWS, DA_WIDTH), F32),
        ],
    )
    return pl.pallas_call(
        functools.partial(_paged_attn_kernel, pages_per_step=pp, t_new=t_new, past_len=past_len, lam_init=lam_init),
        out_shape=jax.ShapeDtypeStruct((n_seq, t_new, DA_WIDTH), F32),
        grid_spec=grid_spec,
        compiler_params=_cparams(3),
        name="diff_attn_sample",
    )(page_table, lamp, row_slopes, q_rep, k_new, v_new, subln_g.reshape(1, -1), *([cache_k] * pp),
      *([cache_v] * pp))


def _mem_attn_kernel(q_ref, mk_ref, mv_ref, o_ref):
    scale = MEM_HEAD_DIM ** -0.5
    q = q_ref[...].astype(BF16)
    mk = mk_ref[...].astype(BF16)
    mv = mv_ref[...].astype(BF16)
    outs = []
    for h in range(MEM_HEADS):
        cols = slice(h * MEM_HEAD_DIM, (h + 1) * MEM_HEAD_DIM)
        s = _dot_nt(q[:, cols], mk[:, cols]) * scale
        e = jnp.exp(s - jnp.max(s, axis=-1, keepdims=True))
        p = e / jnp.sum(e, axis=-1, keepdims=True)
        outs.append(_dot(p.astype(BF16), mv[:, cols]))
    o_ref[...] = jnp.concatenate(outs, axis=1).astype(o_ref.dtype)


def mem_attn(q_arr, mk, mv, *, n_batch, rows_per_batch, tq, q_col0):
    nq = rows_per_batch // tq
    return pl.pallas_call(
        _mem_attn_kernel,
        out_shape=jax.ShapeDtypeStruct((n_batch * rows_per_batch, MEM_WIDTH), BF16),
        grid=(n_batch, nq),
        in_specs=[
            pl.BlockSpec((tq, MEM_WIDTH), lambda b, i: (b * nq + i, q_col0)),
            pl.BlockSpec((None, N_MEM, MEM_WIDTH), lambda b, i: (b, 0, 0)),
            pl.BlockSpec((None, N_MEM, MEM_WIDTH), lambda b, i: (b, 0, 0)),
        ],
        out_specs=pl.BlockSpec((tq, MEM_WIDTH), lambda b, i: (b * nq + i, 0)),
        compiler_params=_cparams(2),
        name="mem_attn",
    )(q_arr, mk, mv)


def _merge_kernel(x_ref, a_ref, b_ref, m_ref, gate_ref, wa_ref, wb_ref, wm_ref, wo_ref, o_ref):
    d = D_MODEL
    ga = gate_ref[:, 0:d]
    gb = gate_ref[:, d:2 * d]
    gm = gate_ref[:, 2 * d:3 * d]
    y = ga.astype(F32) * _dot(a_ref[...].astype(BF16), wa_ref[...])
    y += gb.astype(F32) * _dot(b_ref[...].astype(BF16), wb_ref[...])
    y += gm.astype(F32) * _dot(m_ref[...].astype(BF16), wm_ref[...])
    o_ref[...] = x_ref[...] + _dot(y.astype(BF16), wo_ref[...])


def merge(x, a, b, mo, gates, wa, wb, wm, wo, *, tm):
    m, d = x.shape
    full = lambda arr: pl.BlockSpec(arr.shape, lambda i: (0, 0))
    rowblk = lambda arr: pl.BlockSpec((tm, arr.shape[1]), lambda i: (i, 0))
    return pl.pallas_call(
        _merge_kernel,
        out_shape=jax.ShapeDtypeStruct((m, d), F32),
        grid=(m // tm,),
        in_specs=[rowblk(x), rowblk(a), rowblk(b), rowblk(mo), rowblk(gates), full(wa), full(wb), full(wm), full(wo)],
        out_specs=pl.BlockSpec((tm, d), lambda i: (i, 0)),
        compiler_params=_cparams(1),
        name="merge",
    )(x, a, b, mo, gates, wa, wb, wm, wo)


def _silu(x):
    return x / (1.0 + jnp.exp(-x))


def _ffn_kernel(x_ref, g_ref, wg_ref, wu_ref, wd_ref, o_ref, h_scr, acc_scr):
    f = pl.program_id(1)

    @pl.when(f == 0)
    def _():
        h_scr[...] = _rms(x_ref[...], g_ref[...]).astype(BF16)
        acc_scr[...] = jnp.zeros(acc_scr.shape, F32)

    h = h_scr[...]
    act = _silu(_dot(h, wg_ref[...])) * _dot(h, wu_ref[...])
    acc_scr[...] += _dot(act.astype(BF16), wd_ref[...])

    @pl.when(f == pl.num_programs(1) - 1)
    def _():
        o_ref[...] = x_ref[...] + acc_scr[...]


def ffn_dense(x, g, wg, wu, wd, *, tm, tf):
    m, d = x.shape
    f = wg.shape[1]
    return pl.pallas_call(
        _ffn_kernel,
        out_shape=jax.ShapeDtypeStruct((m, d), F32),
        grid=(m // tm, f // tf),
        in_specs=[
            pl.BlockSpec((tm, d), lambda i, j: (i, 0)),
            pl.BlockSpec((1, d), lambda i, j: (0, 0)),
            pl.BlockSpec((d, tf), lambda i, j: (0, j)),
            pl.BlockSpec((d, tf), lambda i, j: (0, j)),
            pl.BlockSpec((tf, d), lambda i, j: (j, 0)),
        ],
        out_specs=pl.BlockSpec((tm, d), lambda i, j: (i, 0)),
        scratch_shapes=[pltpu.VMEM((tm, d), BF16), pltpu.VMEM((tm, d), F32)],
        compiler_params=_cparams(2),
        name="ffn_dense",
    )(x, g.reshape(1, d), wg, wu, wd)


def _router_kernel(x_ref, g_ref, wr_ref, br_ref, idx_ref, gate_ref):
    h = _rms(x_ref[...], g_ref[...]).astype(BF16)
    logits = _dot(h, wr_ref[...]) + br_ref[...]
    lane_i = lax.broadcasted_iota(jnp.int32, logits.shape, 1)
    lane = lane_i.astype(F32)
    big = float(ROUTER_LANES)
    t1 = jnp.max(logits, axis=-1, keepdims=True)
    i1 = jnp.min(jnp.where(logits == t1, lane, big), axis=-1, keepdims=True)
    rest = jnp.where(lane == i1, NEG_INF, logits)
    t2 = jnp.max(rest, axis=-1, keepdims=True)
    i2 = jnp.min(jnp.where(rest == t2, lane, big), axis=-1, keepdims=True)
    e2 = jnp.exp(t2 - t1)
    g1 = 1.0 / (1.0 + e2)
    idx_ref[...] = jnp.where(lane_i == 0, i1, i2).astype(jnp.int32)
    gate_ref[...] = jnp.where(lane_i == 0, g1, e2 * g1)


def moe_router(x, g, w_router_pad, b_router_pad, *, tm):
    m, d = x.shape
    wide = pl.BlockSpec((tm, ROUTER_LANES), lambda i: (i, 0))
    return pl.pallas_call(
        _router_kernel,
        out_shape=(jax.ShapeDtypeStruct((m, ROUTER_LANES), jnp.int32),
                   jax.ShapeDtypeStruct((m, ROUTER_LANES), F32)),
        grid=(m // tm,),
        in_specs=[
            pl.BlockSpec((tm, d), lambda i: (i, 0)),
            pl.BlockSpec((1, d), lambda i: (0, 0)),
            pl.BlockSpec((d, ROUTER_LANES), lambda i: (0, 0)),
            pl.BlockSpec((1, ROUTER_LANES), lambda i: (0, 0)),
        ],
        out_specs=(wide, wide),
        compiler_params=_cparams(1),
        name="moe_router",
    )(x, g.reshape(1, d), w_router_pad, b_router_pad)


def _moe_ffn_kernel(be_ref, nused_ref, tok_ref, tok_next_ref, x_hbm, g_ref, wg_ref, wu_ref, wd_ref, o_ref, xbuf,
                    h_scr, acc_scr, sem):
    b = pl.program_id(0)
    f = pl.program_id(1)
    n_used = nused_ref[0]
    bm = xbuf.shape[1]

    def row_copy(tok, r, slot):
        return pltpu.make_async_copy(x_hbm.at[pl.ds(tok[0, r], 1), :], xbuf.at[slot, pl.ds(r, 1), :], sem.at[slot])

    def start_rows(tok, slot):
        def body(r, carry):
            row_copy(tok, r, slot).start()
            return carry
        lax.fori_loop(0, bm, body, 0)

    def wait_rows(tok, slot):
        def body(r, carry):
            row_copy(tok, r, slot).wait()
            return carry
        lax.fori_loop(0, bm, body, 0)

    @pl.when(b < n_used)
    def _():
        @pl.when(f == 0)
        def _():
            slot = b % 2

            @pl.when(b == 0)
            def _():
                start_rows(tok_ref, 0)

            wait_rows(tok_ref, slot)

            @pl.when(b + 1 < n_used)
            def _():
                start_rows(tok_next_ref, 1 - slot)

            h_scr[...] = _rms(xbuf[slot], g_ref[...]).astype(BF16)
            acc_scr[...] = jnp.zeros(acc_scr.shape, F32)

        h = h_scr[...]
        act = _silu(_dot(h, wg_ref[...])) * _dot(h, wu_ref[...])
        acc_scr[...] += _dot(act.astype(BF16), wd_ref[...])

        @pl.when(f == pl.num_programs(1) - 1)
        def _():
            o_ref[...] = acc_scr[...].astype(o_ref.dtype)

    @pl.when(jnp.logical_and(b >= n_used, f == pl.num_programs(1) - 1))
    def _():
        o_ref[...] = jnp.zeros(o_ref.shape, o_ref.dtype)


def moe_ffn_grouped(block_expert, n_used, tok_buf, x, g, wg, wu, wd, *, bm, tf, out_dtype):
    l = tok_buf.shape[0]
    d = x.shape[1]
    n_blocks = l // bm
    n_f = wg.shape[2] // tf
    tok3 = tok_buf.reshape(n_blocks, 1, bm)

    def fidx(b, f, nused):
        return jnp.where(b < nused[0], f, n_f - 1)

    grid_spec = pltpu.PrefetchScalarGridSpec(
        num_scalar_prefetch=2,
        grid=(n_blocks, n_f),
        in_specs=[
            pl.BlockSpec((None, 1, bm), lambda b, f, be, nu: (b, 0, 0), memory_space=pltpu.SMEM),
            pl.BlockSpec((None, 1, bm), lambda b, f, be, nu: (jnp.minimum(b + 1, n_blocks - 1), 0, 0),
                         memory_space=pltpu.SMEM),
            pl.BlockSpec(memory_space=pl.ANY),
            pl.BlockSpec((1, d), lambda b, f, be, nu: (0, 0)),
            pl.BlockSpec((None, d, tf), lambda b, f, be, nu: (be[b], 0, fidx(b, f, nu))),
            pl.BlockSpec((None, d, tf), lambda b, f, be, nu: (be[b], 0, fidx(b, f, nu))),
            pl.BlockSpec((None, tf, d), lambda b, f, be, nu: (be[b], fidx(b, f, nu), 0)),
        ],
        out_specs=pl.BlockSpec((bm, d), lambda b, f, be, nu: (b, 0)),
        scratch_shapes=[
            pltpu.VMEM((2, bm, d), F32),
            pltpu.VMEM((bm, d), BF16),
            pltpu.VMEM((bm, d), F32),
            pltpu.SemaphoreType.DMA((2,)),
        ],
    )
    return pl.pallas_call(
        _moe_ffn_kernel,
        out_shape=jax.ShapeDtypeStruct((l, d), out_dtype),
        grid_spec=grid_spec,
        compiler_params=_cparams(2),
        name="moe_ffn_grouped",
    )(block_expert, n_used, tok3, tok3, x, g.reshape(1, d), wg, wu, wd)


def _combine_norm_kernel(x_ref, y0_ref, y1_ref, gate_ref, g_ref, xo_ref, yo_ref):
    gate = gate_ref[...]
    x = x_ref[...] + gate[:, 0:1] * y0_ref[...].astype(F32) + gate[:, 1:2] * y1_ref[...].astype(F32)
    xo_ref[...] = x
    yo_ref[...] = _rms(x, g_ref[...])


def combine_norm(x, y0, y1, gates, g_final, *, tm):
    m, d = x.shape
    blk = pl.BlockSpec((tm, d), lambda i: (i, 0))
    return pl.pallas_call(
        _combine_norm_kernel,
        out_shape=(jax.ShapeDtypeStruct((m, d), F32), jax.ShapeDtypeStruct((m, d), F32)),
        grid=(m // tm,),
        in_specs=[blk, blk, blk, pl.BlockSpec((tm, ROUTER_LANES), lambda i: (i, 0)),
                  pl.BlockSpec((1, d), lambda i: (0, 0))],
        out_specs=(blk, blk),
        compiler_params=_cparams(1),
        name="combine_norm",
    )(x, y0, y1, gates, g_final.reshape(1, d))


def moe_layer(x, g, w_router, b_router, wg, wu, wd, g_final, *, tm, bm, tf, y_dtype):
    m, d = x.shape
    wr = jnp.zeros((d, ROUTER_LANES), BF16).at[:, :N_EXPERTS].set(w_router.astype(BF16))
    br = jnp.full((1, ROUTER_LANES), NEG_INF, F32).at[0, :N_EXPERTS].set(b_router)
    idx, gates = moe_router(x, g, wr, br, tm=tm)

    nk = m * TOP_K
    flat_e = idx[:, :TOP_K].reshape(nk)
    onehot = (flat_e[:, None] == jnp.arange(N_EXPERTS, dtype=jnp.int32)[None, :]).astype(jnp.int32)
    csum = jnp.cumsum(onehot, axis=0)
    counts = csum[-1]
    rank = jnp.take_along_axis(csum, flat_e[:, None], axis=1)[:, 0] - 1
    padded = (counts + bm - 1) // bm * bm
    pad_end = jnp.cumsum(padded)
    pad_start = pad_end - padded
    dest = pad_start[flat_e] + rank
    l_rows = (nk + bm - 1) // bm * bm + N_EXPERTS * bm
    n_blocks = l_rows // bm
    tok_buf = jnp.zeros((l_rows,), jnp.int32).at[dest].set(jnp.arange(nk, dtype=jnp.int32) // TOP_K)
    block_expert = jnp.clip(
        jnp.searchsorted(pad_end, jnp.arange(n_blocks, dtype=jnp.int32) * bm, side="right"), 0, N_EXPERTS - 1
    ).astype(jnp.int32)
    n_used = (pad_end[-1] // bm).astype(jnp.int32).reshape(1)

    y_buf = moe_ffn_grouped(block_expert, n_used, tok_buf, x, g, wg, wu, wd, bm=bm, tf=tf, out_dtype=y_dtype)
    dest2 = dest.reshape(m, TOP_K)
    y0 = jnp.take(y_buf, dest2[:, 0], axis=0)
    y1 = jnp.take(y_buf, dest2[:, 1], axis=0)
    return combine_norm(x, y0, y1, gates, g_final, tm=tm)


def _sample_mix_weights(sgu_w, sgu_b, t_new, n_seq):
    r = jnp.arange(n_seq * t_new)
    t = r % t_new
    same = (r[:, None] // t_new) == (r[None, :] // t_new)
    causal = t[None, :] <= t[:, None]
    w = sgu_w[:, t[:, None], t[None, :]] * (same & causal)[None].astype(sgu_w.dtype)
    bias = jnp.repeat(sgu_b[:, t].T, CHUNK, axis=1)
    return w, bias


def kernel(x_prompt, x_sample, mem_prompt, cache_attn_k, cache_attn_v, cache_mem_k, cache_mem_v, page_table, norm_mix, w_in, b_gate, sgu_ln_g, sgu_ln_b, sgu_w, sgu_b, lam_q1, lam_k1, lam_q2, lam_k2, subln_g, mem_norm, w_mem_k, w_mem_v, w_br_a, w_br_b, w_br_m, w_out, norm_ffn, w_d_gate, w_d_up, w_d_down, w_router, b_router, w_e_gate, w_e_up, w_e_down, norm_final):
    batch, seq, d = x_prompt.shape
    n_seq, t_new, _ = x_sample.shape
    depth = w_in.shape[0]
    n_pages = page_table.shape[1]
    past_len = n_pages * PAGE_SIZE
    mp = batch * seq
    ms = n_seq * t_new
    assert ms == CHUNK and d == D_MODEL and N_QROWS == 2 * DA_HEADS * t_new

    xp = x_prompt.reshape(mp, d)
    xs = x_sample.reshape(ms, d)
    mem = mem_prompt.reshape(batch * N_MEM, d)
    ck = cache_attn_k.reshape(cache_attn_k.shape[0], cache_attn_k.shape[1], PAGE_SIZE * DA_HEADS, DA_V_DIM)
    cv = cache_attn_v.reshape(cache_attn_v.shape[0], cache_attn_v.shape[1], PAGE_SIZE * DA_HEADS, DA_V_DIM)
    slopes =jnp.asarray([2.0 ** (-8.0 * (i + 1) / DA_HEADS) for i in range(DA_HEADS)], F32)
    row_slopes = jnp.repeat(slopes, 2 * t_new).reshape(N_QROWS, 1)
    tril = jnp.tril(jnp.ones((CHUNK, CHUNK), F32))

    c_q = 2 * SGU_WIDTH
    c_k = c_q + DA_WIDTH
    c_qm = c_k + 2 * DA_WIDTH
    c_gate = c_qm + MEM_WIDTH

    outs = {k: [] for k in ("pk", "pv", "pmk", "pmv", "sk", "sv", "scv")}
    yp = ys = None
    for l in range(depth):
        lam_init = 0.8 - 0.6 * math.exp(-0.3 * l)
        lamp = jnp.stack([lam_q1[l], lam_k1[l], lam_q2[l], lam_k2[l]])
        w_l = w_in[l]
        w_z = jnp.concatenate([w_l[:, :c_k], w_l[:, c_qm:c_gate]], axis=1).astype(BF16)
        w_kv = w_l[:, c_k:c_qm].astype(BF16)
        w_g = w_l[:, c_gate:].astype(BF16)
        zero_bias = jnp.zeros((w_z.shape[1],), F32)
        w_mem = jnp.concatenate([w_mem_k[l], w_mem_v[l]], axis=1).astype(BF16)
        wa, wb, wm, wo = (w.astype(BF16) for w in (w_br_a[l], w_br_b[l], w_br_m[l], w_out[l]))
        w_mix_p = (sgu_w[l] * tril[None]).astype(BF16)
        bias_p = jnp.repeat(sgu_b[l].T, CHUNK, axis=1)
        w_mix_s, bias_s = _sample_mix_weights(sgu_w[l], sgu_b[l], t_new, n_seq)
        qm_blk = c_k // MEM_WIDTH

        mkv = norm_matmul(mem, mem_norm[l], w_mem, jnp.zeros((2 * MEM_WIDTH,), F32), sigmoid=False,
                          out_dtype=F32, tm=batch * N_MEM, tn=MEM_WIDTH)
        mk_p = mkv[:, :MEM_WIDTH].reshape(batch, N_MEM, MEM_WIDTH)
        mv_p = mkv[:, MEM_WIDTH:].reshape(batch, N_MEM, MEM_WIDTH)

        z = norm_matmul(xp, norm_mix[l], w_z, zero_bias, sigmoid=False, out_dtype=BF16, tm=1024, tn=512)
        kf, vf, kb, vt = norm_kv(xp, norm_mix[l], w_kv, tm=512, batch=batch)
        gates = norm_matmul(xp, norm_mix[l], w_g, b_gate[l], sigmoid=True, out_dtype=BF16, tm=1024, tn=512)
        (a,) = sgu(z, sgu_ln_g[l], sgu_ln_b[l], w_mix_p, bias_p, n_chunks=4, emit_v=False)
        bo = diff_attn_prompt(z, kb, vt, slopes, lamp, subln_g[l], batch=batch, seq=seq, lam_init=lam_init,
                              q_col0=c_q // DA_V_DIM, tq=1024)
        mo = mem_attn(z, mk_p, mv_p, n_batch=batch, rows_per_batch=seq, tq=512, q_col0=qm_blk)
        xp = merge(xp, a, bo, mo, gates, wa, wb, wm, wo, tm=512)

        zs = norm_matmul(xs, norm_mix[l], w_z, zero_bias, sigmoid=False, out_dtype=F32, tm=ms, tn=512)
        kfs, vfs, _, _ = norm_kv(xs, norm_mix[l], w_kv, tm=ms, batch=1)
        gates_s = norm_matmul(xs, norm_mix[l], w_g, b_gate[l], sigmoid=True, out_dtype=F32, tm=ms, tn=512)
        a_s, vrows = sgu(zs, sgu_ln_g[l], sgu_ln_b[l], w_mix_s.astype(BF16), bias_s, n_chunks=1, emit_v=True)
        q_s = zs[:, c_q:c_k].reshape(n_seq, 1, t_new, DA_WIDTH)
        q_rep = jnp.broadcast_to(q_s, (n_seq, 2 * DA_HEADS, t_new, DA_WIDTH)).reshape(n_seq, N_QROWS, DA_WIDTH)
        bo_s = diff_attn_sample(page_table, q_rep, kfs.reshape(n_seq, t_new, DA_WIDTH),
                                vfs.reshape(n_seq, t_new, DA_WIDTH), ck, cv, lamp, row_slopes, subln_g[l], layer=l,
                                past_len=past_len, lam_init=lam_init, pages_per_step=8)
        qm_s = zs[:, c_k:c_k + MEM_WIDTH].reshape(n_seq, t_new, MEM_WIDTH)
        qm_s = jnp.pad(qm_s, ((0, 0), (0, QM_PAD_ROWS - t_new), (0, 0))).reshape(n_seq * QM_PAD_ROWS, MEM_WIDTH)
        mo_s = mem_attn(qm_s, cache_mem_k[l].reshape(n_seq, N_MEM, MEM_WIDTH),
                        cache_mem_v[l].reshape(n_seq, N_MEM, MEM_WIDTH), n_batch=n_seq,
                        rows_per_batch=QM_PAD_ROWS, tq=QM_PAD_ROWS, q_col0=0)
        mo_s = mo_s.reshape(n_seq, QM_PAD_ROWS, MEM_WIDTH)[:, :t_new].reshape(ms, MEM_WIDTH)
        xs = merge(xs, a_s, bo_s.reshape(ms, DA_WIDTH), mo_s, gates_s, wa, wb, wm, wo, tm=ms)

        if l % 2 == 0:
            i = l // 2
            wg, wu, wd = w_d_gate[i].astype(BF16), w_d_up[i].astype(BF16), w_d_down[i].astype(BF16)
            xp = ffn_dense(xp, norm_ffn[l], wg, wu, wd, tm=512, tf=1408)
            xs = ffn_dense(xs, norm_ffn[l], wg, wu, wd, tm=ms, tf=1408)
        else:
            i = l // 2
            wg, wu, wd = w_e_gate[i].astype(BF16), w_e_up[i].astype(BF16), w_e_down[i].astype(BF16)
            xp, yp = moe_layer(xp, norm_ffn[l], w_router[i], b_router[i], wg, wu, wd, norm_final, tm=512, bm=512,
                               tf=896, y_dtype=BF16)
            xs, ys = moe_layer(xs, norm_ffn[l], w_router[i], b_router[i], wg, wu, wd, norm_final, tm=ms, bm=128,
                               tf=896, y_dtype=F32)

        outs["pk"].append(kf.reshape(batch, seq, DA_HEADS, DA_V_DIM))
        outs["pv"].append(vf.reshape(batch, seq, DA_HEADS, DA_V_DIM))
        outs["pmk"].append(mk_p.reshape(batch, N_MEM, MEM_HEADS, MEM_HEAD_DIM))
        outs["pmv"].append(mv_p.reshape(batch, N_MEM, MEM_HEADS, MEM_HEAD_DIM))
        outs["sk"].append(kfs.reshape(n_seq, t_new, DA_HEADS, DA_V_DIM))
        outs["sv"].append(vfs.reshape(n_seq, t_new, DA_HEADS, DA_V_DIM))
        outs["scv"].append(vrows.reshape(n_seq, t_new, SGU_WIDTH))

    if depth % 2 == 1:
        raise NotImplementedError("final norm is fused into the expert layer, which must come last")
    return (yp.reshape(batch, seq, d), ys.reshape(n_seq, t_new, d),
            jnp.stack(outs["pk"]), jnp.stack(outs["pv"]), jnp.stack(outs["pmk"]), jnp.stack(outs["pmv"]),
            jnp.stack(outs["sk"]), jnp.stack(outs["sv"]), jnp.stack(outs["scv"]))
```

```python
import functools
import math

import jax
import jax.numpy as jnp
from jax import lax
from jax.experimental import pallas as pl
from jax.experimental.pallas import tpu as pltpu

F32 = jnp.float32
BF16 = jnp.bfloat16

D_MODEL = 1024
CHUNK = 128
SGU_GROUPS = 4
SGU_WIDTH = 512
DA_HEADS = 8
DA_QK_DIM = 64
DA_V_DIM = 128
DA_WIDTH = DA_HEADS * DA_V_DIM
MEM_HEADS = 4
MEM_HEAD_DIM = 128
MEM_WIDTH = MEM_HEADS * MEM_HEAD_DIM
N_MEM = 256
N_EXPERTS = 8
TOP_K = 2
PAGE_SIZE = 128
EPS = 1e-5
NEG_INF = float("-inf")

VMEM_LIMIT_BYTES = 48 * 1024 * 1024
LANES = 128
ROUTER_LANES = 128
QM_PAD_ROWS = 16
ALIBI_SPLIT = 16
SUM_ROWS = 16
LOG2E = 1.4426950408889634
ROW_DMA_UNROLL = 8


def _cparams(n_axes):
    return pltpu.CompilerParams(
        dimension_semantics=("arbitrary",) * n_axes, vmem_limit_bytes=VMEM_LIMIT_BYTES)


def _rms(x, g):
    ms = jnp.mean(x * x, axis=-1, keepdims=True)
    return x * lax.rsqrt(ms + EPS) * g


def _gelu(x):
    return 0.5 * x * (1.0 + lax.erf(x * (1.0 / math.sqrt(2.0))))


def _dot(a, b):
    return jnp.dot(a, b, preferred_element_type=F32)


def _dot_nt(a, b):
    return lax.dot_general(a, b, (((1,), (1,)), ((), ())), preferred_element_type=F32)


def _norm_matmul_kernel(x_ref, g_ref, w_ref, b_ref, o_ref, h_scr, *, sigmoid):
    @pl.when(pl.program_id(1) == 0)
    def _():
        h_scr[...] = _rms(x_ref[...], g_ref[...]).astype(BF16)

    acc = _dot(h_scr[...], w_ref[...])
    if sigmoid:
        acc = 1.0 / (1.0 + jnp.exp(-(acc + b_ref[...])))
    o_ref[...] = acc.astype(o_ref.dtype)


def norm_matmul(x, g, w, bias, *, sigmoid, out_dtype, tm, tn):
    m, d = x.shape
    n = w.shape[1]
    return pl.pallas_call(
        functools.partial(_norm_matmul_kernel, sigmoid=sigmoid),
        out_shape=jax.ShapeDtypeStruct((m, n), out_dtype),
        grid=(m // tm, n // tn),
        in_specs=[
            pl.BlockSpec((tm, d), lambda i, j: (i, 0)),
            pl.BlockSpec((1, d), lambda i, j: (0, 0)),
            pl.BlockSpec((d, tn), lambda i, j: (0, j)),
            pl.BlockSpec((1, tn), lambda i, j: (0, j)),
        ],
        out_specs=pl.BlockSpec((tm, tn), lambda i, j: (i, j)),
        scratch_shapes=[pltpu.VMEM((tm, d), BF16)],
        compiler_params=_cparams(2),
        name="norm_matmul_sig" if sigmoid else "norm_matmul",
    )(x, g.reshape(1, d), w, bias.reshape(1, n))


def _norm_kv_kernel(x_ref, g_ref, w_ref, kf_ref, vf_ref, kb_ref, vt_ref, h_scr):
    j = pl.program_id(1)

    @pl.when(j == 0)
    def _():
        h_scr[...] = _rms(x_ref[...], g_ref[...]).astype(BF16)

    acc = _dot(h_scr[...], w_ref[...])

    @pl.when(j == 0)
    def _():
        kf_ref[...] = acc
        kb_ref[...] = acc.astype(BF16)

    @pl.when(j == 1)
    def _():
        vf_ref[...] = acc
        vt_ref[...] = acc.T.astype(BF16)


def norm_kv(x, g, w_kv, *, tm, batch):
    m, d = x.shape
    n = w_kv.shape[1] // 2
    nb = m // batch // tm
    blk = pl.BlockSpec((tm, n), lambda i, j: (i, 0))
    return pl.pallas_call(
        _norm_kv_kernel,
        out_shape=(jax.ShapeDtypeStruct((m, n), F32), jax.ShapeDtypeStruct((m, n), F32),
                   jax.ShapeDtypeStruct((m, n), BF16), jax.ShapeDtypeStruct((batch, n, m // batch), BF16)),
        grid=(m // tm, 2),
        in_specs=[
            pl.BlockSpec((tm, d), lambda i, j: (i, 0)),
            pl.BlockSpec((1, d), lambda i, j: (0, 0)),
            pl.BlockSpec((d, n), lambda i, j: (0, j)),
        ],
        out_specs=(blk, blk, blk, pl.BlockSpec((None, n, tm), lambda i, j: (i // nb, 0, i % nb))),
        scratch_shapes=[pltpu.VMEM((tm, d), BF16)],
        compiler_params=_cparams(2),
        name="norm_kv",
    )(x, g.reshape(1, d), w_kv)


def _sgu_kernel(u_ref, v_ref, lng_ref, lnb_ref, w_ref, bias_ref, a_ref, *maybe_vn_ref, n_chunks):
    u = _gelu(u_ref[...].astype(F32))
    v = _gelu(v_ref[...].astype(F32))
    mu = jnp.mean(v, axis=-1, keepdims=True)
    vc = v - mu
    var = jnp.mean(vc * vc, axis=-1, keepdims=True)
    vn = vc * lax.rsqrt(var + EPS) * lng_ref[...] + lnb_ref[...]
    if maybe_vn_ref:
        maybe_vn_ref[0][...] = vn
    vnb = vn.astype(BF16)
    bias = bias_ref[...]
    for c in range(n_chunks):
        rows = slice(c * CHUNK, (c + 1) * CHUNK)
        for g in range(SGU_GROUPS):
            cols = slice(g * CHUNK, (g + 1) * CHUNK)
            mixed = _dot(w_ref[g], vnb[rows, cols]) + bias[:, cols]
            a_ref[rows, cols] = (u[rows, cols] * mixed).astype(a_ref.dtype)


def sgu(z, ln_g, ln_b, w_mix, bias_full, *, n_chunks, emit_v):
    m = z.shape[0]
    rows = n_chunks * CHUNK
    row_blk = pl.BlockSpec((rows, SGU_WIDTH), lambda i: (i, 0))
    out_shape = [jax.ShapeDtypeStruct((m, SGU_WIDTH), BF16)]
    out_specs = [row_blk]
    if emit_v:
        out_shape.append(jax.ShapeDtypeStruct((m, SGU_WIDTH), F32))
        out_specs.append(row_blk)
    return pl.pallas_call(
        functools.partial(_sgu_kernel, n_chunks=n_chunks),
        out_shape=tuple(out_shape),
        grid=(m // rows,),
        in_specs=[
            row_blk,
            pl.BlockSpec((rows, SGU_WIDTH), lambda i: (i, 1)),
            pl.BlockSpec((1, SGU_WIDTH), lambda i: (0, 0)),
            pl.BlockSpec((1, SGU_WIDTH), lambda i: (0, 0)),
            pl.BlockSpec((SGU_GROUPS, CHUNK, CHUNK), lambda i: (0, 0, 0)),
            pl.BlockSpec((CHUNK, SGU_WIDTH), lambda i: (0, 0)),
        ],
        out_specs=tuple(out_specs),
        compiler_params=_cparams(1),
        name="sgu",
    )(z, z, ln_g.reshape(1, -1), ln_b.reshape(1, -1), w_mix, bias_full)


def _diff_lambda(lamp, lam_init):
    s1 = jnp.sum(lamp[0:1, :] * lamp[1:2, :], axis=-1, keepdims=True)
    s2 = jnp.sum(lamp[2:3, :] * lamp[3:4, :], axis=-1, keepdims=True)
    return jnp.exp(s1) - jnp.exp(s2) + lam_init


def _sub_norm(o, g, lam_init):
    return _rms(o, g) * (1.0 - lam_init)


def _diff_attn_kernel(slopes_ref, lamp_ref, q_ref, k_ref, vt_ref, g_ref, o_ref, causal_scr, *, tq, lam_init):
    h = pl.program_id(1)
    qi = pl.program_id(2)
    slope2 = slopes_ref[h] * LOG2E

    @pl.when((pl.program_id(0) == 0) & (h == 0) & (qi == 0))
    def _():
        krow = lax.broadcasted_iota(jnp.int32, (tq, tq), 0)
        qcol = lax.broadcasted_iota(jnp.int32, (tq, tq), 1)
        causal_scr[...] = jnp.where(krow <= qcol, 0.0, NEG_INF)

    q = (q_ref[...].astype(F32) * (DA_QK_DIM ** -0.5 * LOG2E)).astype(BF16)
    lane = lax.broadcasted_iota(jnp.int32, (tq, LANES), 1)
    pos = lax.broadcasted_iota(jnp.int32, (tq, LANES), 0)
    hi = (pos // ALIBI_SPLIT).astype(F32)
    lo = (pos % ALIBI_SPLIT).astype(F32)
    c_hi = slope2.astype(BF16).astype(F32)
    c_lo = (slope2 - c_hi).astype(BF16).astype(F32)
    q_alibi = jnp.zeros((tq, LANES), F32)
    k_alibi = jnp.zeros((tq, LANES), F32)
    for n, c in enumerate((c_hi, c_lo)):
        q_alibi = jnp.where(lane == 4 * n, c * ALIBI_SPLIT, jnp.where(lane == 4 * n + 1, c, jnp.where(
            lane == 4 * n + 2, -c * ALIBI_SPLIT * hi, jnp.where(lane == 4 * n + 3, -c * lo, q_alibi))))
        k_alibi = jnp.where(lane == 4 * n, hi, jnp.where(lane == 4 * n + 1, lo, jnp.where(
            (lane == 4 * n + 2) | (lane == 4 * n + 3), 1.0, k_alibi)))
    q_alibi = q_alibi.astype(BF16)
    k_alibi = k_alibi.astype(BF16)
    qzero = jnp.zeros_like(q)
    qz = (jnp.concatenate([jnp.where(lane < DA_QK_DIM, q, qzero), q_alibi], axis=1),
          jnp.concatenate([jnp.where(lane >= DA_QK_DIM, q, qzero), q_alibi], axis=1))
    ones = jnp.ones((SUM_ROWS, tq), BF16)

    def update(carry, start, mask, shift):
        kt = jnp.concatenate([k_ref[pl.ds(start, tq), :], k_alibi], axis=1)
        vt = jnp.concatenate([vt_ref[:, pl.ds(start, tq)], ones], axis=0)
        ts = [_dot_nt(kt, qm) for qm in qz]
        new = []
        for (m, acc), t in zip(carry, ts):
            if mask is not None:
                t = t + mask
            m_new = jnp.maximum(m, jnp.max(t, axis=0, keepdims=True) + shift)
            p = jnp.exp2(t - (m_new - shift))
            acc_new = jnp.exp2(m - m_new) * acc + _dot(vt, p.astype(BF16))
            new.append((m_new, acc_new))
        return tuple(new)

    def body(j, carry):
        shift = ((j - qi) * tq).astype(F32) * slope2
        return update(carry, pl.multiple_of(j * tq, tq), None, shift)

    one = (jnp.full((1, tq), NEG_INF, F32), jnp.zeros((DA_V_DIM + SUM_ROWS, tq), F32))
    carry = lax.fori_loop(0, qi, body, (one, one))
    (_, a1), (_, a2) = update(carry, pl.multiple_of(qi * tq, tq), causal_scr[...], 0.0)
    lam = _diff_lambda(lamp_ref[...], lam_init)
    o1 = a1[:DA_V_DIM] / a1[DA_V_DIM:DA_V_DIM + 1]
    o2 = a2[:DA_V_DIM] / a2[DA_V_DIM:DA_V_DIM + 1]
    o = (o1 - lam * o2).T
    o_ref[...] = _sub_norm(o, g_ref[...], lam_init).astype(o_ref.dtype)


def diff_attn_prompt(z, kb, vt, slopes, lamp, subln_g, *, batch, seq, lam_init, q_col0, tq):
    nq = seq // tq
    grid_spec = pltpu.PrefetchScalarGridSpec(
        num_scalar_prefetch=1,
        grid=(batch, DA_HEADS, nq),
        in_specs=[
            pl.BlockSpec((4, DA_QK_DIM), lambda b, h, i, s: (0, 0)),
            pl.BlockSpec((tq, DA_V_DIM), lambda b, h, i, s: (b * nq + i, q_col0 + h)),
            pl.BlockSpec((seq, DA_V_DIM), lambda b, h, i, s: (b, h)),
            pl.BlockSpec((None, DA_V_DIM, seq), lambda b, h, i, s: (b, h, 0)),
            pl.BlockSpec((1, DA_V_DIM), lambda b, h, i, s: (0, 0)),
        ],
        out_specs=pl.BlockSpec((tq, DA_V_DIM), lambda b, h, i, s: (b * nq + i, h)),
        scratch_shapes=[pltpu.VMEM((tq, tq), F32)],
    )
    return pl.pallas_call(
        functools.partial(_diff_attn_kernel, tq=tq, lam_init=lam_init),
        out_shape=jax.ShapeDtypeStruct((batch * seq, DA_WIDTH), BF16),
        grid_spec=grid_spec,
        compiler_params=_cparams(3),
        name="diff_attn_prompt",
    )(slopes, lamp, z, kb, vt, subln_g.reshape(1, -1))


N_QROWS = DA_HEADS * 2 * 4


def _paged_attn_kernel(pt_ref, lamp_ref, slope_ref, q_ref, kn_ref, vn_ref, g_ref, ck_hbm, cv_hbm, o_ref, kv_buf,
                       qz_scr, s_scr, sn_scr, m_scr, l_scr, acc_scr, sem, *, pages_per_step, t_new, past_len,
                       lam_init, layer):
    pp = pages_per_step
    b = pl.program_id(0)
    phase = pl.program_id(1)
    g = pl.program_id(2)
    n_seq = pl.num_programs(0)
    n_groups = pl.num_programs(2)
    last = n_groups - 1
    step_keys = pp * PAGE_SIZE
    rows_per_head = 2 * t_new
    slot = (phase * n_groups + g) % 2

    def tile_copy(cache_hbm, seq, grp, r, h, dst_slot):
        page_id = pt_ref[seq, grp * pp + r]
        return pltpu.make_async_copy(cache_hbm.at[layer, page_id, :, h, :], kv_buf.at[dst_slot, r, h],
                                     sem.at[dst_slot])

    def start_step(seq, ph, grp, dst_slot):
        for cache_hbm, which in ((ck_hbm, 0), (cv_hbm, 1)):
            @pl.when(ph == which)
            def _():
                for r in range(pp):
                    for h in range(DA_HEADS):
                        tile_copy(cache_hbm, seq, grp, r, h, dst_slot).start()

    def wait_step():
        for cache_hbm, which in ((ck_hbm, 0), (cv_hbm, 1)):
            @pl.when(phase == which)
            def _():
                for r in range(pp):
                    for h in range(DA_HEADS):
                        tile_copy(cache_hbm, b, g, r, h, slot).wait()

    @pl.when((b == 0) & (phase == 0) & (g == 0))
    def _():
        start_step(b, phase, g, slot)

    wrap = g == last
    nxt_g = jnp.where(wrap, 0, g + 1)
    nxt_phase = jnp.where(wrap, 1 - phase, phase)
    nxt_b = jnp.where(wrap & (phase == 1), b + 1, b)

    @pl.when(nxt_b < n_seq)
    def _():
        start_step(nxt_b, nxt_phase, nxt_g, 1 - slot)

    wait_step()

    rowi = lax.broadcasted_iota(jnp.int32, (N_QROWS, 1), 0)
    slope = slope_ref[...]
    tok = rowi % t_new
    lam = _diff_lambda(lamp_ref[...], lam_init)

    def page(r):
        return jnp.concatenate([kv_buf[slot, r, h].astype(BF16) for h in range(DA_HEADS)], axis=1)

    def combine(p):
        return pltpu.roll(p, t_new, axis=0) - lam * p

    def new_key_scores():
        kn = jnp.concatenate([kn_ref[...].astype(BF16), jnp.zeros((PAGE_SIZE - t_new, DA_WIDTH), BF16)], axis=0)
        s_all = _dot_nt(qz_scr[...], kn)
        row = lax.broadcasted_iota(jnp.int32, s_all.shape, 0)
        j = lax.broadcasted_iota(jnp.int32, s_all.shape, 1)
        valid = (j < t_new) & (j <= row % t_new)
        return jnp.where(valid, s_all - slope * (row % t_new - j).astype(F32), NEG_INF)

    @pl.when(jnp.logical_and(phase == 0, g == 0))
    def _():
        row = lax.broadcasted_iota(jnp.int32, (N_QROWS, DA_WIDTH), 0)
        lane = lax.broadcasted_iota(jnp.int32, (N_QROWS, DA_WIDTH), 1)
        keep = (lane // DA_QK_DIM) == (row // t_new)
        qz_scr[...] = jnp.where(keep, q_ref[...] * (DA_QK_DIM ** -0.5), 0.0).astype(BF16)
        m_scr[...] = jnp.full(m_scr.shape, NEG_INF, F32)
        l_scr[...] = jnp.zeros(l_scr.shape, F32)
        acc_scr[...] = jnp.zeros(acc_scr.shape, F32)

    def online(s):
        m_old = m_scr[...]
        m_new = jnp.maximum(m_old, jnp.max(s, axis=-1, keepdims=True))
        l_scr[...] = jnp.exp(m_old - m_new) * l_scr[...] + jnp.sum(jnp.exp(s - m_new), axis=-1, keepdims=True)
        m_scr[...] = m_new

    @pl.when(phase == 0)
    def _():
        qz = qz_scr[...]
        s = jnp.concatenate([_dot_nt(qz, page(r)) for r in range(pp)], axis=1)
        kpos = g * step_keys + lax.broadcasted_iota(jnp.int32, (1, step_keys), 1)
        s = s - slope * ((past_len + tok) - kpos).astype(F32)
        s_scr[:, pl.ds(pl.multiple_of(g * step_keys, step_keys), step_keys)] = s
        online(s)

        @pl.when(g == last)
        def _():
            sn = new_key_scores()
            sn_scr[...] = sn
            online(sn)

    @pl.when(phase == 1)
    def _():
        s = s_scr[:, pl.ds(pl.multiple_of(g * step_keys, step_keys), step_keys)]
        m = m_scr[...]
        inv_l = 1.0 / l_scr[...]
        pc = combine(jnp.exp(s - m) * inv_l).astype(BF16)
        pv = _dot(pc[:, 0:PAGE_SIZE], page(0))
        for r in range(1, pp):
            pv += _dot(pc[:, r * PAGE_SIZE:(r + 1) * PAGE_SIZE], page(r))
        acc_scr[...] += pv

        @pl.when(g == last)
        def _():
            pn = combine(jnp.exp(sn_scr[...] - m) * inv_l).astype(BF16)
            vn = jnp.concatenate([vn_ref[...].astype(BF16), jnp.zeros((PAGE_SIZE - t_new, DA_WIDTH), BF16)], axis=0)
            acc = acc_scr[...] + _dot(pn, vn)
            outs = []
            for h in range(DA_HEADS):
                r0 = h * rows_per_head + t_new
                outs.append(_sub_norm(acc[r0:r0 + t_new, h * DA_V_DIM:(h + 1) * DA_V_DIM], g_ref[...], lam_init))
            o_ref[...] = jnp.concatenate(outs, axis=1)


def diff_attn_sample(page_table, q_rep, k_new, v_new, cache_k, cache_v, lamp, row_slopes, subln_g, *, layer,
                     past_len, lam_init, pages_per_step):
    n_seq, t_new = k_new.shape[:2]
    n_pages = page_table.shape[1]
    pp = pages_per_step
    n_groups = n_pages // pp
    assert n_pages % pp == 0
    new_spec = pl.BlockSpec((None, t_new, DA_WIDTH), lambda b, ph, g, pt: (b, 0, 0))
    grid_spec = pltpu.PrefetchScalarGridSpec(
        num_scalar_prefetch=1,
        grid=(n_seq, 2, n_groups),
        in_specs=[
            pl.BlockSpec((4, DA_QK_DIM), lambda b, ph, g, pt: (0, 0)),
            pl.BlockSpec((N_QROWS, 1), lambda b, ph, g, pt: (0, 0)),
            pl.BlockSpec((None, N_QROWS, DA_WIDTH), lambda b, ph, g, pt: (b, 0, 0)),
            new_spec, new_spec,
            pl.BlockSpec((1, DA_V_DIM), lambda b, ph, g, pt: (0, 0)),
            pl.BlockSpec(memory_space=pl.ANY),
            pl.BlockSpec(memory_space=pl.ANY),
        ],
        out_specs=pl.BlockSpec((None, t_new, DA_WIDTH), lambda b, ph, g, pt: (b, 0, 0)),
        scratch_shapes=[
            pltpu.VMEM((2, pp, DA_HEADS, PAGE_SIZE, DA_V_DIM), F32),
            pltpu.VMEM((N_QROWS, DA_WIDTH), BF16),
            pltpu.VMEM((N_QROWS, n_pages * PAGE_SIZE), F32),
            pltpu.VMEM((N_QROWS, PAGE_SIZE), F32),
            pltpu.VMEM((N_QROWS, 1), F32),
            pltpu.VMEM((N_QROWS, 1), F32),
            pltpu.VMEM((N_QROWS, DA_WIDTH), F32),
            pltpu.SemaphoreType.DMA((2,)),
        ],
    )
    return pl.pallas_call(
        functools.partial(_paged_attn_kernel, pages_per_step=pp, t_new=t_new, past_len=past_len, lam_init=lam_init,
                          layer=layer),
        out_shape=jax.ShapeDtypeStruct((n_seq, t_new, DA_WIDTH), F32),
        grid_spec=grid_spec,
        compiler_params=_cparams(3),
        name="diff_attn_sample",
    )(page_table, lamp, row_slopes, q_rep, k_new, v_new, subln_g.reshape(1, -1), cache_k, cache_v)


def _mem_attn_kernel(q_ref, mk_ref, mv_ref, o_ref):
    scale = MEM_HEAD_DIM ** -0.5
    q = q_ref[...].astype(BF16)
    mk = mk_ref[...].astype(BF16)
    mv = mv_ref[...].astype(BF16)
    outs = []
    for h in range(MEM_HEADS):
        cols = slice(h * MEM_HEAD_DIM, (h + 1) * MEM_HEAD_DIM)
        s = _dot_nt(q[:, cols], mk[:, cols]) * scale
        e = jnp.exp(s - jnp.max(s, axis=-1, keepdims=True))
        p = e / jnp.sum(e, axis=-1, keepdims=True)
        outs.append(_dot(p.astype(BF16), mv[:, cols]))
    o_ref[...] = jnp.concatenate(outs, axis=1).astype(o_ref.dtype)


def mem_attn(q_arr, mk, mv, *, n_batch, rows_per_batch, tq, q_col0):
    nq = rows_per_batch // tq
    return pl.pallas_call(
        _mem_attn_kernel,
        out_shape=jax.ShapeDtypeStruct((n_batch * rows_per_batch, MEM_WIDTH), BF16),
        grid=(n_batch, nq),
        in_specs=[
            pl.BlockSpec((tq, MEM_WIDTH), lambda b, i: (b * nq + i, q_col0)),
            pl.BlockSpec((None, N_MEM, MEM_WIDTH), lambda b, i: (b, 0, 0)),
            pl.BlockSpec((None, N_MEM, MEM_WIDTH), lambda b, i: (b, 0, 0)),
        ],
        out_specs=pl.BlockSpec((tq, MEM_WIDTH), lambda b, i: (b * nq + i, 0)),
        compiler_params=_cparams(2),
        name="mem_attn",
    )(q_arr, mk, mv)


def _merge_kernel(x_ref, a_ref, b_ref, m_ref, gate_ref, wa_ref, wb_ref, wm_ref, wo_ref, o_ref):
    d = D_MODEL
    ga = gate_ref[:, 0:d]
    gb = gate_ref[:, d:2 * d]
    gm = gate_ref[:, 2 * d:3 * d]
    y = ga.astype(F32) * _dot(a_ref[...].astype(BF16), wa_ref[...])
    y += gb.astype(F32) * _dot(b_ref[...].astype(BF16), wb_ref[...])
    y += gm.astype(F32) * _dot(m_ref[...].astype(BF16), wm_ref[...])
    o_ref[...] = x_ref[...] + _dot(y.astype(BF16), wo_ref[...])


def merge(x, a, b, mo, gates, wa, wb, wm, wo, *, tm):
    m, d = x.shape
    full = lambda arr: pl.BlockSpec(arr.shape, lambda i: (0, 0))
    rowblk = lambda arr: pl.BlockSpec((tm, arr.shape[1]), lambda i: (i, 0))
    return pl.pallas_call(
        _merge_kernel,
        out_shape=jax.ShapeDtypeStruct((m, d), F32),
        grid=(m // tm,),
        in_specs=[rowblk(x), rowblk(a), rowblk(b), rowblk(mo), rowblk(gates), full(wa), full(wb), full(wm), full(wo)],
        out_specs=pl.BlockSpec((tm, d), lambda i: (i, 0)),
        compiler_params=_cparams(1),
        name="merge",
    )(x, a, b, mo, gates, wa, wb, wm, wo)


def _silu(x):
    return x / (1.0 + jnp.exp(-x))


def _ffn_kernel(x_ref, g_ref, wg_ref, wu_ref, wd_ref, o_ref, h_scr, acc_scr):
    f = pl.program_id(1)

    @pl.when(f == 0)
    def _():
        h_scr[...] = _rms(x_ref[...], g_ref[...]).astype(BF16)
        acc_scr[...] = jnp.zeros(acc_scr.shape, F32)

    h = h_scr[...]
    act = _silu(_dot(h, wg_ref[...])) * _dot(h, wu_ref[...])
    acc_scr[...] += _dot(act.astype(BF16), wd_ref[...])

    @pl.when(f == pl.num_programs(1) - 1)
    def _():
        o_ref[...] = x_ref[...] + acc_scr[...]


def ffn_dense(x, g, wg, wu, wd, *, tm, tf):
    m, d = x.shape
    f = wg.shape[1]
    return pl.pallas_call(
        _ffn_kernel,
        out_shape=jax.ShapeDtypeStruct((m, d), F32),
        grid=(m // tm, f // tf),
        in_specs=[
            pl.BlockSpec((tm, d), lambda i, j: (i, 0)),
            pl.BlockSpec((1, d), lambda i, j: (0, 0)),
            pl.BlockSpec((d, tf), lambda i, j: (0, j)),
            pl.BlockSpec((d, tf), lambda i, j: (0, j)),
            pl.BlockSpec((tf, d), lambda i, j: (j, 0)),
        ],
        out_specs=pl.BlockSpec((tm, d), lambda i, j: (i, 0)),
        scratch_shapes=[pltpu.VMEM((tm, d), BF16), pltpu.VMEM((tm, d), F32)],
        compiler_params=_cparams(2),
        name="ffn_dense",
    )(x, g.reshape(1, d), wg, wu, wd)


def _router_kernel(x_ref, g_ref, wr_ref, br_ref, idx_ref, gate_ref):
    h = _rms(x_ref[...], g_ref[...]).astype(BF16)
    logits = _dot(h, wr_ref[...]) + br_ref[...]
    lane_i = lax.broadcasted_iota(jnp.int32, logits.shape, 1)
    lane = lane_i.astype(F32)
    big = float(ROUTER_LANES)
    t1 = jnp.max(logits, axis=-1, keepdims=True)
    i1 = jnp.min(jnp.where(logits == t1, lane, big), axis=-1, keepdims=True)
    rest = jnp.where(lane == i1, NEG_INF, logits)
    t2 = jnp.max(rest, axis=-1, keepdims=True)
    i2 = jnp.min(jnp.where(rest == t2, lane, big), axis=-1, keepdims=True)
    e2 = jnp.exp(t2 - t1)
    g1 = 1.0 / (1.0 + e2)
    idx_ref[...] = jnp.where(lane_i == 0, i1, i2).astype(jnp.int32)
    gate_ref[...] = jnp.where(lane_i == 0, g1, e2 * g1)


def moe_router(x, g, w_router_pad, b_router_pad, *, tm):
    m, d = x.shape
    wide = pl.BlockSpec((tm, ROUTER_LANES), lambda i: (i, 0))
    return pl.pallas_call(
        _router_kernel,
        out_shape=(jax.ShapeDtypeStruct((m, ROUTER_LANES), jnp.int32),
                   jax.ShapeDtypeStruct((m, ROUTER_LANES), F32)),
        grid=(m // tm,),
        in_specs=[
            pl.BlockSpec((tm, d), lambda i: (i, 0)),
            pl.BlockSpec((1, d), lambda i: (0, 0)),
            pl.BlockSpec((d, ROUTER_LANES), lambda i: (0, 0)),
            pl.BlockSpec((1, ROUTER_LANES), lambda i: (0, 0)),
        ],
        out_specs=(wide, wide),
        compiler_params=_cparams(1),
        name="moe_router",
    )(x, g.reshape(1, d), w_router_pad, b_router_pad)


def _moe_ffn_kernel(be_ref, nused_ref, tok_ref, tok_next_ref, x_hbm, g_ref, wg_ref, wu_ref, wd_ref, o_ref, xbuf,
                    h_scr, acc_scr, sem):
    b = pl.program_id(0)
    f = pl.program_id(1)
    n_used = nused_ref[0]
    bm = xbuf.shape[1]

    def row_copy(tok, r, slot):
        return pltpu.make_async_copy(x_hbm.at[pl.ds(tok[0, r], 1), :], xbuf.at[slot, pl.ds(r, 1), :], sem.at[slot])

    def start_rows(tok, slot):
        def body(i, carry):
            for u in range(ROW_DMA_UNROLL):
                row_copy(tok, i * ROW_DMA_UNROLL + u, slot).start()
            return carry
        lax.fori_loop(0, bm // ROW_DMA_UNROLL, body, 0)

    def wait_rows(tok, slot):
        pltpu.make_async_copy(x_hbm.at[pl.ds(0, bm), :], xbuf.at[slot], sem.at[slot]).wait()

    @pl.when(b < n_used)
    def _():
        @pl.when(f == 0)
        def _():
            slot = b % 2

            @pl.when(b == 0)
            def _():
                start_rows(tok_ref, 0)

            wait_rows(tok_ref, slot)

            @pl.when(b + 1 < n_used)
            def _():
                start_rows(tok_next_ref, 1 - slot)

            h_scr[...] = _rms(xbuf[slot], g_ref[...]).astype(BF16)
            acc_scr[...] = jnp.zeros(acc_scr.shape, F32)

        h = h_scr[...]
        act = _silu(_dot(h, wg_ref[...])) * _dot(h, wu_ref[...])
        acc_scr[...] += _dot(act.astype(BF16), wd_ref[...])

        @pl.when(f == pl.num_programs(1) - 1)
        def _():
            o_ref[...] = acc_scr[...].astype(o_ref.dtype)

    @pl.when(jnp.logical_and(b >= n_used, f == pl.num_programs(1) - 1))
    def _():
        o_ref[...] = jnp.zeros(o_ref.shape, o_ref.dtype)


def moe_ffn_grouped(block_expert, n_used, tok_buf, x, g, wg, wu, wd, *, bm, tf, out_dtype):
    l = tok_buf.shape[0]
    d = x.shape[1]
    n_blocks = l // bm
    n_f = wg.shape[2] // tf
    tok3 = tok_buf.reshape(n_blocks, 1, bm)

    def fidx(b, f, nused):
        return jnp.where(b < nused[0], f, n_f - 1)

    grid_spec = pltpu.PrefetchScalarGridSpec(
        num_scalar_prefetch=2,
        grid=(n_blocks, n_f),
        in_specs=[
            pl.BlockSpec((None, 1, bm), lambda b, f, be, nu: (b, 0, 0), memory_space=pltpu.SMEM),
            pl.BlockSpec((None, 1, bm), lambda b, f, be, nu: (jnp.minimum(b + 1, n_blocks - 1), 0, 0),
                         memory_space=pltpu.SMEM),
            pl.BlockSpec(memory_space=pl.ANY),
            pl.BlockSpec((1, d), lambda b, f, be, nu: (0, 0)),
            pl.BlockSpec((None, d, tf), lambda b, f, be, nu: (be[b], 0, fidx(b, f, nu))),
            pl.BlockSpec((None, d, tf), lambda b, f, be, nu: (be[b], 0, fidx(b, f, nu))),
            pl.BlockSpec((None, tf, d), lambda b, f, be, nu: (be[b], fidx(b, f, nu), 0)),
        ],
        out_specs=pl.BlockSpec((bm, d), lambda b, f, be, nu: (b, 0)),
        scratch_shapes=[
            pltpu.VMEM((2, bm, d), F32),
            pltpu.VMEM((bm, d), BF16),
            pltpu.VMEM((bm, d), F32),
            pltpu.SemaphoreType.DMA((2,)),
        ],
    )
    return pl.pallas_call(
        _moe_ffn_kernel,
        out_shape=jax.ShapeDtypeStruct((l, d), out_dtype),
        grid_spec=grid_spec,
        compiler_params=_cparams(2),
        name="moe_ffn_grouped",
    )(block_expert, n_used, tok3, tok3, x, g.reshape(1, d), wg, wu, wd)


def _combine_norm_kernel(x_ref, y0_ref, y1_ref, gate_ref, g_ref, xo_ref, yo_ref):
    gate = gate_ref[...]
    x = x_ref[...] + gate[:, 0:1] * y0_ref[...].astype(F32) + gate[:, 1:2] * y1_ref[...].astype(F32)
    xo_ref[...] = x
    yo_ref[...] = _rms(x, g_ref[...])


def combine_norm(x, y0, y1, gates, g_final, *, tm):
    m, d = x.shape
    blk = pl.BlockSpec((tm, d), lambda i: (i, 0))
    return pl.pallas_call(
        _combine_norm_kernel,
        out_shape=(jax.ShapeDtypeStruct((m, d), F32), jax.ShapeDtypeStruct((m, d), F32)),
        grid=(m // tm,),
        in_specs=[blk, blk, blk, pl.BlockSpec((tm, ROUTER_LANES), lambda i: (i, 0)),
                  pl.BlockSpec((1, d), lambda i: (0, 0))],
        out_specs=(blk, blk),
        compiler_params=_cparams(1),
        name="combine_norm",
    )(x, y0, y1, gates, g_final.reshape(1, d))


def moe_layer(x, g, w_router, b_router, wg, wu, wd, g_final, *, tm, bm, tf, y_dtype):
    m, d = x.shape
    wr = jnp.zeros((d, ROUTER_LANES), BF16).at[:, :N_EXPERTS].set(w_router.astype(BF16))
    br = jnp.full((1, ROUTER_LANES), NEG_INF, F32).at[0, :N_EXPERTS].set(b_router)
    idx, gates = moe_router(x, g, wr, br, tm=tm)

    nk = m * TOP_K
    flat_e = idx[:, :TOP_K].reshape(nk)
    onehot = (flat_e[:, None] == jnp.arange(N_EXPERTS, dtype=jnp.int32)[None, :]).astype(jnp.int32)
    csum = jnp.cumsum(onehot, axis=0)
    counts = csum[-1]
    rank = jnp.take_along_axis(csum, flat_e[:, None], axis=1)[:, 0] - 1
    padded = (counts + bm - 1) // bm * bm
    pad_end = jnp.cumsum(padded)
    pad_start = pad_end - padded
    dest = pad_start[flat_e] + rank
    l_rows = (nk + bm - 1) // bm * bm + N_EXPERTS * bm
    n_blocks = l_rows // bm
    tok_buf = jnp.zeros((l_rows,), jnp.int32).at[dest].set(jnp.arange(nk, dtype=jnp.int32) // TOP_K)
    block_expert = jnp.clip(
        jnp.searchsorted(pad_end, jnp.arange(n_blocks, dtype=jnp.int32) * bm, side="right"), 0, N_EXPERTS - 1
    ).astype(jnp.int32)
    n_used = (pad_end[-1] // bm).astype(jnp.int32).reshape(1)

    y_buf = moe_ffn_grouped(block_expert, n_used, tok_buf, x, g, wg, wu, wd, bm=bm, tf=tf, out_dtype=y_dtype)
    dest2 = dest.reshape(m, TOP_K)
    y0 = jnp.take(y_buf, dest2[:, 0], axis=0)
    y1 = jnp.take(y_buf, dest2[:, 1], axis=0)
    return combine_norm(x, y0, y1, gates, g_final, tm=tm)


def _sample_mix_weights(sgu_w, sgu_b, t_new, n_seq):
    r = jnp.arange(n_seq * t_new)
    t = r % t_new
    same = (r[:, None] // t_new) == (r[None, :] // t_new)
    causal = t[None, :] <= t[:, None]
    w = sgu_w[:, t[:, None], t[None, :]] * (same & causal)[None].astype(sgu_w.dtype)
    bias = jnp.repeat(sgu_b[:, t].T, CHUNK, axis=1)
    return w, bias


def kernel(x_prompt, x_sample, mem_prompt, cache_attn_k, cache_attn_v, cache_mem_k, cache_mem_v, page_table, norm_mix, w_in, b_gate, sgu_ln_g, sgu_ln_b, sgu_w, sgu_b, lam_q1, lam_k1, lam_q2, lam_k2, subln_g, mem_norm, w_mem_k, w_mem_v, w_br_a, w_br_b, w_br_m, w_out, norm_ffn, w_d_gate, w_d_up, w_d_down, w_router, b_router, w_e_gate, w_e_up, w_e_down, norm_final):
    batch, seq, d = x_prompt.shape
    n_seq, t_new, _ = x_sample.shape
    depth = w_in.shape[0]
    n_pages = page_table.shape[1]
    past_len = n_pages * PAGE_SIZE
    mp = batch * seq
    ms = n_seq * t_new
    assert ms == CHUNK and d == D_MODEL and N_QROWS == 2 * DA_HEADS * t_new

    xp = x_prompt.reshape(mp, d)
    xs = x_sample.reshape(ms, d)
    mem = mem_prompt.reshape(batch * N_MEM, d)
    slopes =jnp.asarray([2.0 ** (-8.0 * (i + 1) / DA_HEADS) for i in range(DA_HEADS)], F32)
    row_slopes = jnp.repeat(slopes, 2 * t_new).reshape(N_QROWS, 1)
    tril = jnp.tril(jnp.ones((CHUNK, CHUNK), F32))

    c_q = 2 * SGU_WIDTH
    c_k = c_q + DA_WIDTH
    c_qm = c_k + 2 * DA_WIDTH
    c_gate = c_qm + MEM_WIDTH

    outs = {k: [] for k in ("pk", "pv", "pmk", "pmv", "sk", "sv", "scv")}
    yp = ys = None
    for l in range(depth):
        lam_init = 0.8 - 0.6 * math.exp(-0.3 * l)
        lamp = jnp.stack([lam_q1[l], lam_k1[l], lam_q2[l], lam_k2[l]])
        w_l = w_in[l]
        w_z = jnp.concatenate([w_l[:, :c_k], w_l[:, c_qm:c_gate]], axis=1).astype(BF16)
        w_kv = w_l[:, c_k:c_qm].astype(BF16)
        w_g = w_l[:, c_gate:].astype(BF16)
        zero_bias = jnp.zeros((w_z.shape[1],), F32)
        w_mem = jnp.concatenate([w_mem_k[l], w_mem_v[l]], axis=1).astype(BF16)
        wa, wb, wm, wo = (w.astype(BF16) for w in (w_br_a[l], w_br_b[l], w_br_m[l], w_out[l]))
        w_mix_p = (sgu_w[l] * tril[None]).astype(BF16)
        bias_p = jnp.repeat(sgu_b[l].T, CHUNK, axis=1)
        w_mix_s, bias_s = _sample_mix_weights(sgu_w[l], sgu_b[l], t_new, n_seq)
        qm_blk = c_k // MEM_WIDTH

        mkv = norm_matmul(mem, mem_norm[l], w_mem, jnp.zeros((2 * MEM_WIDTH,), F32), sigmoid=False,
                          out_dtype=F32, tm=batch * N_MEM, tn=MEM_WIDTH)
        mk_p = mkv[:, :MEM_WIDTH].reshape(batch, N_MEM, MEM_WIDTH)
        mv_p = mkv[:, MEM_WIDTH:].reshape(batch, N_MEM, MEM_WIDTH)

        z = norm_matmul(xp, norm_mix[l], w_z, zero_bias, sigmoid=False, out_dtype=BF16, tm=1024, tn=512)
        kf, vf, kb, vt = norm_kv(xp, norm_mix[l], w_kv, tm=512, batch=batch)
        gates = norm_matmul(xp, norm_mix[l], w_g, b_gate[l], sigmoid=True, out_dtype=BF16, tm=1024, tn=512)
        (a,) = sgu(z, sgu_ln_g[l], sgu_ln_b[l], w_mix_p, bias_p, n_chunks=4, emit_v=False)
        bo = diff_attn_prompt(z, kb, vt, slopes, lamp, subln_g[l], batch=batch, seq=seq, lam_init=lam_init,
                              q_col0=c_q // DA_V_DIM, tq=1024)
        mo = mem_attn(z, mk_p, mv_p, n_batch=batch, rows_per_batch=seq, tq=512, q_col0=qm_blk)
        xp = merge(xp, a, bo, mo, gates, wa, wb, wm, wo, tm=512)

        zs = norm_matmul(xs, norm_mix[l], w_z, zero_bias, sigmoid=False, out_dtype=F32, tm=ms, tn=512)
        kfs, vfs, _, _ = norm_kv(xs, norm_mix[l], w_kv, tm=ms, batch=1)
        gates_s = norm_matmul(xs, norm_mix[l], w_g, b_gate[l], sigmoid=True, out_dtype=F32, tm=ms, tn=512)
        a_s, vrows = sgu(zs, sgu_ln_g[l], sgu_ln_b[l], w_mix_s.astype(BF16), bias_s, n_chunks=1, emit_v=True)
        q_s = zs[:, c_q:c_k].reshape(n_seq, 1, t_new, DA_WIDTH)
        q_rep = jnp.broadcast_to(q_s, (n_seq, 2 * DA_HEADS, t_new, DA_WIDTH)).reshape(n_seq, N_QROWS, DA_WIDTH)
        bo_s = diff_attn_sample(page_table, q_rep, kfs.reshape(n_seq, t_new, DA_WIDTH),
                                vfs.reshape(n_seq, t_new, DA_WIDTH), cache_attn_k, cache_attn_v, lamp, row_slopes,
                                subln_g[l], layer=l, past_len=past_len, lam_init=lam_init, pages_per_step=8)
        qm_s = zs[:, c_k:c_k + MEM_WIDTH].reshape(n_seq, t_new, MEM_WIDTH)
        qm_s = jnp.pad(qm_s, ((0, 0), (0, QM_PAD_ROWS - t_new), (0, 0))).reshape(n_seq * QM_PAD_ROWS, MEM_WIDTH)
        mo_s = mem_attn(qm_s, cache_mem_k[l].reshape(n_seq, N_MEM, MEM_WIDTH),
                        cache_mem_v[l].reshape(n_seq, N_MEM, MEM_WIDTH), n_batch=n_seq,
                        rows_per_batch=QM_PAD_ROWS, tq=QM_PAD_ROWS, q_col0=0)
        mo_s = mo_s.reshape(n_seq, QM_PAD_ROWS, MEM_WIDTH)[:, :t_new].reshape(ms, MEM_WIDTH)
        xs = merge(xs, a_s, bo_s.reshape(ms, DA_WIDTH), mo_s, gates_s, wa, wb, wm, wo, tm=ms)

        if l % 2 == 0:
            i = l // 2
            wg, wu, wd = w_d_gate[i].astype(BF16), w_d_up[i].astype(BF16), w_d_down[i].astype(BF16)
            xp = ffn_dense(xp, norm_ffn[l], wg, wu, wd, tm=512, tf=1408)
            xs = ffn_dense(xs, norm_ffn[l], wg, wu, wd, tm=ms, tf=1408)
        else:
            i = l // 2
            wg, wu, wd = w_e_gate[i].astype(BF16), w_e_up[i].astype(BF16), w_e_down[i].astype(BF16)
            xp, yp = moe_layer(xp, norm_ffn[l], w_router[i], b_router[i], wg, wu, wd, norm_final, tm=512, bm=512,
                               tf=896, y_dtype=BF16)
            xs, ys = moe_layer(xs, norm_ffn[l], w_router[i], b_router[i], wg, wu, wd, norm_final, tm=ms, bm=128,
                               tf=896, y_dtype=F32)

        outs["pk"].append(kf.reshape(batch, seq, DA_HEADS, DA_V_DIM))
        outs["pv"].append(vf.reshape(batch, seq, DA_HEADS, DA_V_DIM))
        outs["pmk"].append(mk_p.reshape(batch, N_MEM, MEM_HEADS, MEM_HEAD_DIM))
        outs["pmv"].append(mv_p.reshape(batch, N_MEM, MEM_HEADS, MEM_HEAD_DIM))
        outs["sk"].append(kfs.reshape(n_seq, t_new, DA_HEADS, DA_V_DIM))
        outs["sv"].append(vfs.reshape(n_seq, t_new, DA_HEADS, DA_V_DIM))
        outs["scv"].append(vrows.reshape(n_seq, t_new, SGU_WIDTH))

    if depth % 2 == 1:
        raise NotImplementedError("final norm is fused into the expert layer, which must come last")
    return (yp.reshape(batch, seq, d), ys.reshape(n_seq, t_new, d),
            jnp.stack(outs["pk"]), jnp.stack(outs["pv"]), jnp.stack(outs["pmk"]), jnp.stack(outs["pmv"]),
            jnp.stack(outs["sk"]), jnp.stack(outs["sv"]), jnp.stack(outs["scv"]))
```

```python
import functools
import math

import jax
import jax.numpy as jnp
from jax import lax
from jax.experimental import pallas as pl
from jax.experimental.pallas import tpu as pltpu

F32 = jnp.float32
BF16 = jnp.bfloat16

D_MODEL = 1024
CHUNK = 128
SGU_GROUPS = 4
SGU_WIDTH = 512
DA_HEADS = 8
DA_QK_DIM = 64
DA_V_DIM = 128
DA_WIDTH = DA_HEADS * DA_V_DIM
MEM_HEADS = 4
MEM_HEAD_DIM = 128
MEM_WIDTH = MEM_HEADS * MEM_HEAD_DIM
N_MEM = 256
N_EXPERTS = 8
TOP_K = 2
PAGE_SIZE = 128
EPS = 1e-5
NEG_INF = float("-inf")

VMEM_LIMIT_BYTES = 48 * 1024 * 1024
LANES = 128
ROUTER_LANES = 128
QM_PAD_ROWS = 16
ALIBI_SPLIT = 16
SUM_ROWS = 16
LOG2E = 1.4426950408889634
ROW_DMA_UNROLL = 8


def _cparams(n_axes):
    return pltpu.CompilerParams(
        dimension_semantics=("arbitrary",) * n_axes, vmem_limit_bytes=VMEM_LIMIT_BYTES)


def _rms(x, g):
    ms = jnp.mean(x * x, axis=-1, keepdims=True)
    return x * lax.rsqrt(ms + EPS) * g


def _gelu(x):
    return 0.5 * x * (1.0 + lax.erf(x * (1.0 / math.sqrt(2.0))))


def _sigmoid(x):
    return 0.5 * (jnp.tanh(0.5 * x) + 1.0)


def _dot(a, b):
    return jnp.dot(a, b, preferred_element_type=F32)


def _dot_nt(a, b):
    return lax.dot_general(a, b, (((1,), (1,)), ((), ())), preferred_element_type=F32)


def _norm_matmul_kernel(x_ref, g_ref, w_ref, b_ref, o_ref, h_scr, *, sigmoid):
    @pl.when(pl.program_id(1) == 0)
    def _():
        h_scr[...] = _rms(x_ref[...], g_ref[...]).astype(BF16)

    acc = _dot(h_scr[...], w_ref[...])
    if sigmoid:
        acc = _sigmoid(acc + b_ref[...])
    o_ref[...] = acc.astype(o_ref.dtype)


def norm_matmul(x, g, w, bias, *, sigmoid, out_dtype, tm, tn):
    m, d = x.shape
    n = w.shape[1]
    return pl.pallas_call(
        functools.partial(_norm_matmul_kernel, sigmoid=sigmoid),
        out_shape=jax.ShapeDtypeStruct((m, n), out_dtype),
        grid=(m // tm, n // tn),
        in_specs=[
            pl.BlockSpec((tm, d), lambda i, j: (i, 0)),
            pl.BlockSpec((1, d), lambda i, j: (0, 0)),
            pl.BlockSpec((d, tn), lambda i, j: (0, j)),
            pl.BlockSpec((1, tn), lambda i, j: (0, j)),
        ],
        out_specs=pl.BlockSpec((tm, tn), lambda i, j: (i, j)),
        scratch_shapes=[pltpu.VMEM((tm, d), BF16)],
        compiler_params=_cparams(2),
        name="norm_matmul_sig" if sigmoid else "norm_matmul",
    )(x, g.reshape(1, d), w, bias.reshape(1, n))


def _norm_kv_kernel(x_ref, g_ref, w_ref, kf_ref, vf_ref, kb_ref, vt_ref, h_scr):
    j = pl.program_id(1)

    @pl.when(j == 0)
    def _():
        h_scr[...] = _rms(x_ref[...], g_ref[...]).astype(BF16)

    acc = _dot(h_scr[...], w_ref[...])

    @pl.when(j == 0)
    def _():
        kf_ref[...] = acc
        kb_ref[...] = acc.astype(BF16)

    @pl.when(j == 1)
    def _():
        vf_ref[...] = acc
        vt_ref[...] = acc.T.astype(BF16)


def norm_kv(x, g, w_kv, *, tm, batch):
    m, d = x.shape
    n = w_kv.shape[1] // 2
    nb = m // batch // tm
    blk = pl.BlockSpec((tm, n), lambda i, j: (i, 0))
    return pl.pallas_call(
        _norm_kv_kernel,
        out_shape=(jax.ShapeDtypeStruct((m, n), F32), jax.ShapeDtypeStruct((m, n), F32),
                   jax.ShapeDtypeStruct((m, n), BF16), jax.ShapeDtypeStruct((batch, n, m // batch), BF16)),
        grid=(m // tm, 2),
        in_specs=[
            pl.BlockSpec((tm, d), lambda i, j: (i, 0)),
            pl.BlockSpec((1, d), lambda i, j: (0, 0)),
            pl.BlockSpec((d, n), lambda i, j: (0, j)),
        ],
        out_specs=(blk, blk, blk, pl.BlockSpec((None, n, tm), lambda i, j: (i // nb, 0, i % nb))),
        scratch_shapes=[pltpu.VMEM((tm, d), BF16)],
        compiler_params=_cparams(2),
        name="norm_kv",
    )(x, g.reshape(1, d), w_kv)


def _sgu_kernel(u_ref, v_ref, lng_ref, lnb_ref, w_ref, bias_ref, a_ref, *maybe_vn_ref, n_chunks):
    u = _gelu(u_ref[...].astype(F32))
    v = _gelu(v_ref[...].astype(F32))
    mu = jnp.mean(v, axis=-1, keepdims=True)
    vc = v - mu
    var = jnp.mean(vc * vc, axis=-1, keepdims=True)
    vn = vc * lax.rsqrt(var + EPS) * lng_ref[...] + lnb_ref[...]
    if maybe_vn_ref:
        maybe_vn_ref[0][...] = vn
    vnb = vn.astype(BF16)
    bias = bias_ref[...]
    for c in range(n_chunks):
        rows = slice(c * CHUNK, (c + 1) * CHUNK)
        for g in range(SGU_GROUPS):
            cols = slice(g * CHUNK, (g + 1) * CHUNK)
            mixed = _dot(w_ref[g], vnb[rows, cols]) + bias[:, cols]
            a_ref[rows, cols] = (u[rows, cols] * mixed).astype(a_ref.dtype)


def sgu(z, ln_g, ln_b, w_mix, bias_full, *, n_chunks, emit_v):
    m = z.shape[0]
    rows = n_chunks * CHUNK
    row_blk = pl.BlockSpec((rows, SGU_WIDTH), lambda i: (i, 0))
    out_shape = [jax.ShapeDtypeStruct((m, SGU_WIDTH), BF16)]
    out_specs = [row_blk]
    if emit_v:
        out_shape.append(jax.ShapeDtypeStruct((m, SGU_WIDTH), F32))
        out_specs.append(row_blk)
    return pl.pallas_call(
        functools.partial(_sgu_kernel, n_chunks=n_chunks),
        out_shape=tuple(out_shape),
        grid=(m // rows,),
        in_specs=[
            row_blk,
            pl.BlockSpec((rows, SGU_WIDTH), lambda i: (i, 1)),
            pl.BlockSpec((1, SGU_WIDTH), lambda i: (0, 0)),
            pl.BlockSpec((1, SGU_WIDTH), lambda i: (0, 0)),
            pl.BlockSpec((SGU_GROUPS, CHUNK, CHUNK), lambda i: (0, 0, 0)),
            pl.BlockSpec((CHUNK, SGU_WIDTH), lambda i: (0, 0)),
        ],
        out_specs=tuple(out_specs),
        compiler_params=_cparams(1),
        name="sgu",
    )(z, z, ln_g.reshape(1, -1), ln_b.reshape(1, -1), w_mix, bias_full)


def _diff_lambda(lamp, lam_init):
    s1 = jnp.sum(lamp[0:1, :] * lamp[1:2, :], axis=-1, keepdims=True)
    s2 = jnp.sum(lamp[2:3, :] * lamp[3:4, :], axis=-1, keepdims=True)
    return jnp.exp(s1) - jnp.exp(s2) + lam_init


def _sub_norm(o, g, lam_init):
    return _rms(o, g) * (1.0 - lam_init)


def _diff_attn_kernel(slopes_ref, lamp_ref, q_ref, k_ref, vt_ref, g_ref, o_ref, causal_scr, q_alibi_scr,
                      k_alibi_scr, *, tq, lam_init):
    h = pl.program_id(1)
    qi = pl.program_id(2)
    slope2 = slopes_ref[h] * LOG2E

    @pl.when((pl.program_id(0) == 0) & (h == 0) & (qi == 0))
    def _():
        krow = lax.broadcasted_iota(jnp.int32, (tq, tq), 0)
        qcol = lax.broadcasted_iota(jnp.int32, (tq, tq), 1)
        causal_scr[...] = jnp.where(krow <= qcol, 0.0, NEG_INF)

    q = (q_ref[...].astype(F32) * (DA_QK_DIM ** -0.5 * LOG2E)).astype(BF16)
    lane = lax.broadcasted_iota(jnp.int32, (tq, LANES), 1)

    @pl.when(qi == 0)
    def _():
        pos = lax.broadcasted_iota(jnp.int32, (tq, LANES), 0)
        hi = (pos // ALIBI_SPLIT).astype(F32)
        lo = (pos % ALIBI_SPLIT).astype(F32)
        c_hi = slope2.astype(BF16).astype(F32)
        c_lo = (slope2 - c_hi).astype(BF16).astype(F32)
        qa = jnp.zeros((tq, LANES), F32)
        ka = jnp.zeros((tq, LANES), F32)
        for n, c in enumerate((c_hi, c_lo)):
            qa = jnp.where(lane == 4 * n, c * ALIBI_SPLIT, jnp.where(lane == 4 * n + 1, c, jnp.where(
                lane == 4 * n + 2, -c * ALIBI_SPLIT * hi, jnp.where(lane == 4 * n + 3, -c * lo, qa))))
            ka = jnp.where(lane == 4 * n, hi, jnp.where(lane == 4 * n + 1, lo, jnp.where(
                (lane == 4 * n + 2) | (lane == 4 * n + 3), 1.0, ka)))
        q_alibi_scr[...] = qa.astype(BF16)
        k_alibi_scr[...] = ka.astype(BF16)

    q_alibi = q_alibi_scr[...]
    k_alibi = k_alibi_scr[...]
    qzero = jnp.zeros_like(q)
    qz = (jnp.concatenate([jnp.where(lane < DA_QK_DIM, q, qzero), q_alibi], axis=1),
          jnp.concatenate([jnp.where(lane >= DA_QK_DIM, q, qzero), q_alibi], axis=1))
    ones = jnp.ones((SUM_ROWS, tq), BF16)

    def update(carry, start, mask, shift):
        kt = jnp.concatenate([k_ref[pl.ds(start, tq), :], k_alibi], axis=1)
        vt = jnp.concatenate([vt_ref[:, pl.ds(start, tq)], ones], axis=0)
        ts = [_dot_nt(kt, qm) for qm in qz]
        new = []
        for (m, acc), t in zip(carry, ts):
            if mask is not None:
                t = t + mask
            m_new = jnp.maximum(m, jnp.max(t, axis=0, keepdims=True) + shift)
            p = jnp.exp2(t - (m_new - shift))
            acc_new = jnp.exp2(m - m_new) * acc + _dot(vt, p.astype(BF16))
            new.append((m_new, acc_new))
        return tuple(new)

    def body(j, carry):
        shift = ((j - qi) * tq).astype(F32) * slope2
        return update(carry, pl.multiple_of(j * tq, tq), None, shift)

    one = (jnp.full((1, tq), NEG_INF, F32), jnp.zeros((DA_V_DIM + SUM_ROWS, tq), F32))
    carry = lax.fori_loop(0, qi, body, (one, one))
    (_, a1), (_, a2) = update(carry, pl.multiple_of(qi * tq, tq), causal_scr[...], 0.0)
    lam = _diff_lambda(lamp_ref[...], lam_init)
    o1 = a1[:DA_V_DIM] / a1[DA_V_DIM:DA_V_DIM + 1]
    o2 = a2[:DA_V_DIM] / a2[DA_V_DIM:DA_V_DIM + 1]
    o = (o1 - lam * o2).T
    o_ref[...] = _sub_norm(o, g_ref[...], lam_init).astype(o_ref.dtype)


def diff_attn_prompt(z, kb, vt, slopes, lamp, subln_g, *, batch, seq, lam_init, q_col0, tq):
    nq = seq // tq
    grid_spec = pltpu.PrefetchScalarGridSpec(
        num_scalar_prefetch=1,
        grid=(batch, DA_HEADS, nq),
        in_specs=[
            pl.BlockSpec((4, DA_QK_DIM), lambda b, h, i, s: (0, 0)),
            pl.BlockSpec((tq, DA_V_DIM), lambda b, h, i, s: (b * nq + i, q_col0 + h)),
            pl.BlockSpec((seq, DA_V_DIM), lambda b, h, i, s: (b, h)),
            pl.BlockSpec((None, DA_V_DIM, seq), lambda b, h, i, s: (b, h, 0)),
            pl.BlockSpec((1, DA_V_DIM), lambda b, h, i, s: (0, 0)),
        ],
        out_specs=pl.BlockSpec((tq, DA_V_DIM), lambda b, h, i, s: (b * nq + i, h)),
        scratch_shapes=[pltpu.VMEM((tq, tq), F32),
                        pltpu.VMEM((tq, LANES), BF16), pltpu.VMEM((tq, LANES), BF16)],
    )
    return pl.pallas_call(
        functools.partial(_diff_attn_kernel, tq=tq, lam_init=lam_init),
        out_shape=jax.ShapeDtypeStruct((batch * seq, DA_WIDTH), BF16),
        grid_spec=grid_spec,
        compiler_params=_cparams(3),
        name="diff_attn_prompt",
    )(slopes, lamp, z, kb, vt, subln_g.reshape(1, -1))


N_QROWS = DA_HEADS * 2 * 4


def _paged_attn_kernel(pt_ref, lamp_ref, slope_ref, q_ref, kn_ref, vn_ref, g_ref, ck_hbm, cv_hbm, o_ref, kv_buf,
                       qz_scr, s_scr, sn_scr, m_scr, l_scr, acc_scr, sem, *, pages_per_step, t_new, past_len,
                       lam_init, layer):
    pp = pages_per_step
    b = pl.program_id(0)
    phase = pl.program_id(1)
    g = pl.program_id(2)
    n_seq = pl.num_programs(0)
    n_groups = pl.num_programs(2)
    last = n_groups - 1
    step_keys = pp * PAGE_SIZE
    rows_per_head = 2 * t_new
    slot = (phase * n_groups + g) % 2

    def tile_copy(cache_hbm, seq, grp, r, h, dst_slot):
        page_id = pt_ref[seq, grp * pp + r]
        return pltpu.make_async_copy(cache_hbm.at[layer, page_id, :, h, :], kv_buf.at[dst_slot, r, h],
                                     sem.at[dst_slot])

    def start_step(seq, ph, grp, dst_slot):
        for cache_hbm, which in ((ck_hbm, 0), (cv_hbm, 1)):
            @pl.when(ph == which)
            def _():
                for r in range(pp):
                    for h in range(DA_HEADS):
                        tile_copy(cache_hbm, seq, grp, r, h, dst_slot).start()

    def wait_step():
        for cache_hbm, which in ((ck_hbm, 0), (cv_hbm, 1)):
            @pl.when(phase == which)
            def _():
                for r in range(pp):
                    for h in range(DA_HEADS):
                        tile_copy(cache_hbm, b, g, r, h, slot).wait()

    @pl.when((b == 0) & (phase == 0) & (g == 0))
    def _():
        start_step(b, phase, g, slot)

    wrap = g == last
    nxt_g = jnp.where(wrap, 0, g + 1)
    nxt_phase = jnp.where(wrap, 1 - phase, phase)
    nxt_b = jnp.where(wrap & (phase == 1), b + 1, b)

    @pl.when(nxt_b < n_seq)
    def _():
        start_step(nxt_b, nxt_phase, nxt_g, 1 - slot)

    wait_step()

    rowi = lax.broadcasted_iota(jnp.int32, (N_QROWS, 1), 0)
    slope = slope_ref[...]
    tok = rowi % t_new
    lam = _diff_lambda(lamp_ref[...], lam_init)

    def page(r):
        return jnp.concatenate([kv_buf[slot, r, h].astype(BF16) for h in range(DA_HEADS)], axis=1)

    def combine(p):
        return pltpu.roll(p, t_new, axis=0) - lam * p

    def new_key_scores():
        kn = jnp.concatenate([kn_ref[...].astype(BF16), jnp.zeros((PAGE_SIZE - t_new, DA_WIDTH), BF16)], axis=0)
        s_all = _dot_nt(qz_scr[...], kn)
        row = lax.broadcasted_iota(jnp.int32, s_all.shape, 0)
        j = lax.broadcasted_iota(jnp.int32, s_all.shape, 1)
        valid = (j < t_new) & (j <= row % t_new)
        return jnp.where(valid, s_all - slope * (row % t_new - j).astype(F32), NEG_INF)

    @pl.when(jnp.logical_and(phase == 0, g == 0))
    def _():
        row = lax.broadcasted_iota(jnp.int32, (N_QROWS, DA_WIDTH), 0)
        lane = lax.broadcasted_iota(jnp.int32, (N_QROWS, DA_WIDTH), 1)
        keep = (lane // DA_QK_DIM) == (row // t_new)
        qz_scr[...] = jnp.where(keep, q_ref[...] * (DA_QK_DIM ** -0.5), 0.0).astype(BF16)
        m_scr[...] = jnp.full(m_scr.shape, NEG_INF, F32)
        l_scr[...] = jnp.zeros(l_scr.shape, F32)
        acc_scr[...] = jnp.zeros(acc_scr.shape, F32)

    def online(s):
        m_old = m_scr[...]
        m_new = jnp.maximum(m_old, jnp.max(s, axis=-1, keepdims=True))
        l_scr[...] = jnp.exp(m_old - m_new) * l_scr[...] + jnp.sum(jnp.exp(s - m_new), axis=-1, keepdims=True)
        m_scr[...] = m_new

    @pl.when(phase == 0)
    def _():
        qz = qz_scr[...]
        s = jnp.concatenate([_dot_nt(qz, page(r)) for r in range(pp)], axis=1)
        kpos = g * step_keys + lax.broadcasted_iota(jnp.int32, (1, step_keys), 1)
        s = s - slope * ((past_len + tok) - kpos).astype(F32)
        s_scr[:, pl.ds(pl.multiple_of(g * step_keys, step_keys), step_keys)] = s
        online(s)

        @pl.when(g == last)
        def _():
            sn = new_key_scores()
            sn_scr[...] = sn
            online(sn)

    @pl.when(phase == 1)
    def _():
        s = s_scr[:, pl.ds(pl.multiple_of(g * step_keys, step_keys), step_keys)]
        m = m_scr[...]
        inv_l = 1.0 / l_scr[...]
        pc = combine(jnp.exp(s - m) * inv_l).astype(BF16)
        pv = _dot(pc[:, 0:PAGE_SIZE], page(0))
        for r in range(1, pp):
            pv += _dot(pc[:, r * PAGE_SIZE:(r + 1) * PAGE_SIZE], page(r))
        acc_scr[...] += pv

        @pl.when(g == last)
        def _():
            pn = combine(jnp.exp(sn_scr[...] - m) * inv_l).astype(BF16)
            vn = jnp.concatenate([vn_ref[...].astype(BF16), jnp.zeros((PAGE_SIZE - t_new, DA_WIDTH), BF16)], axis=0)
            acc = acc_scr[...] + _dot(pn, vn)
            outs = []
            for h in range(DA_HEADS):
                r0 = h * rows_per_head + t_new
                outs.append(_sub_norm(acc[r0:r0 + t_new, h * DA_V_DIM:(h + 1) * DA_V_DIM], g_ref[...], lam_init))
            o_ref[...] = jnp.concatenate(outs, axis=1)


def diff_attn_sample(page_table, q_rep, k_new, v_new, cache_k, cache_v, lamp, row_slopes, subln_g, *, layer,
                     past_len, lam_init, pages_per_step):
    n_seq, t_new = k_new.shape[:2]
    n_pages = page_table.shape[1]
    pp = pages_per_step
    n_groups = n_pages // pp
    assert n_pages % pp == 0
    new_spec = pl.BlockSpec((None, t_new, DA_WIDTH), lambda b, ph, g, pt: (b, 0, 0))
    grid_spec = pltpu.PrefetchScalarGridSpec(
        num_scalar_prefetch=1,
        grid=(n_seq, 2, n_groups),
        in_specs=[
            pl.BlockSpec((4, DA_QK_DIM), lambda b, ph, g, pt: (0, 0)),
            pl.BlockSpec((N_QROWS, 1), lambda b, ph, g, pt: (0, 0)),
            pl.BlockSpec((None, N_QROWS, DA_WIDTH), lambda b, ph, g, pt: (b, 0, 0)),
            new_spec, new_spec,
            pl.BlockSpec((1, DA_V_DIM), lambda b, ph, g, pt: (0, 0)),
            pl.BlockSpec(memory_space=pl.ANY),
            pl.BlockSpec(memory_space=pl.ANY),
        ],
        out_specs=pl.BlockSpec((None, t_new, DA_WIDTH), lambda b, ph, g, pt: (b, 0, 0)),
        scratch_shapes=[
            pltpu.VMEM((2, pp, DA_HEADS, PAGE_SIZE, DA_V_DIM), F32),
            pltpu.VMEM((N_QROWS, DA_WIDTH), BF16),
            pltpu.VMEM((N_QROWS, n_pages * PAGE_SIZE), F32),
            pltpu.VMEM((N_QROWS, PAGE_SIZE), F32),
            pltpu.VMEM((N_QROWS, 1), F32),
            pltpu.VMEM((N_QROWS, 1), F32),
            pltpu.VMEM((N_QROWS, DA_WIDTH), F32),
            pltpu.SemaphoreType.DMA((2,)),
        ],
    )
    return pl.pallas_call(
        functools.partial(_paged_attn_kernel, pages_per_step=pp, t_new=t_new, past_len=past_len, lam_init=lam_init,
                          layer=layer),
        out_shape=jax.ShapeDtypeStruct((n_seq, t_new, DA_WIDTH), F32),
        grid_spec=grid_spec,
        compiler_params=_cparams(3),
        name="diff_attn_sample",
    )(page_table, lamp, row_slopes, q_rep, k_new, v_new, subln_g.reshape(1, -1), cache_k, cache_v)


def _mem_attn_kernel(q_ref, mk_ref, mv_ref, o_ref):
    scale = MEM_HEAD_DIM ** -0.5
    q = q_ref[...].astype(BF16)
    mk = mk_ref[...].astype(BF16)
    mv = mv_ref[...].astype(BF16)
    outs = []
    for h in range(MEM_HEADS):
        cols = slice(h * MEM_HEAD_DIM, (h + 1) * MEM_HEAD_DIM)
        s = _dot_nt(q[:, cols], mk[:, cols]) * scale
        e = jnp.exp(s - jnp.max(s, axis=-1, keepdims=True))
        p = e / jnp.sum(e, axis=-1, keepdims=True)
        outs.append(_dot(p.astype(BF16), mv[:, cols]))
    o_ref[...] = jnp.concatenate(outs, axis=1).astype(o_ref.dtype)


def mem_attn(q_arr, mk, mv, *, n_batch, rows_per_batch, tq, q_col0):
    nq = rows_per_batch // tq
    return pl.pallas_call(
        _mem_attn_kernel,
        out_shape=jax.ShapeDtypeStruct((n_batch * rows_per_batch, MEM_WIDTH), BF16),
        grid=(n_batch, nq),
        in_specs=[
            pl.BlockSpec((tq, MEM_WIDTH), lambda b, i: (b * nq + i, q_col0)),
            pl.BlockSpec((None, N_MEM, MEM_WIDTH), lambda b, i: (b, 0, 0)),
            pl.BlockSpec((None, N_MEM, MEM_WIDTH), lambda b, i: (b, 0, 0)),
        ],
        out_specs=pl.BlockSpec((tq, MEM_WIDTH), lambda b, i: (b * nq + i, 0)),
        compiler_params=_cparams(2),
        name="mem_attn",
    )(q_arr, mk, mv)


def _merge_kernel(x_ref, a_ref, b_ref, m_ref, gate_ref, wa_ref, wb_ref, wm_ref, wo_ref, o_ref):
    d = D_MODEL
    ga = gate_ref[:, 0:d]
    gb = gate_ref[:, d:2 * d]
    gm = gate_ref[:, 2 * d:3 * d]
    y = ga.astype(F32) * _dot(a_ref[...].astype(BF16), wa_ref[...])
    y += gb.astype(F32) * _dot(b_ref[...].astype(BF16), wb_ref[...])
    y += gm.astype(F32) * _dot(m_ref[...].astype(BF16), wm_ref[...])
    o_ref[...] = x_ref[...] + _dot(y.astype(BF16), wo_ref[...])


def merge(x, a, b, mo, gates, wa, wb, wm, wo, *, tm):
    m, d = x.shape
    full = lambda arr: pl.BlockSpec(arr.shape, lambda i: (0, 0))
    rowblk = lambda arr: pl.BlockSpec((tm, arr.shape[1]), lambda i: (i, 0))
    return pl.pallas_call(
        _merge_kernel,
        out_shape=jax.ShapeDtypeStruct((m, d), F32),
        grid=(m // tm,),
        in_specs=[rowblk(x), rowblk(a), rowblk(b), rowblk(mo), rowblk(gates), full(wa), full(wb), full(wm), full(wo)],
        out_specs=pl.BlockSpec((tm, d), lambda i: (i, 0)),
        compiler_params=_cparams(1),
        name="merge",
    )(x, a, b, mo, gates, wa, wb, wm, wo)


def _silu(x):
    return x * _sigmoid(x)


def _ffn_kernel(x_ref, g_ref, wg_ref, wu_ref, wd_ref, o_ref, h_scr, acc_scr):
    f = pl.program_id(1)

    @pl.when(f == 0)
    def _():
        h_scr[...] = _rms(x_ref[...], g_ref[...]).astype(BF16)
        acc_scr[...] = jnp.zeros(acc_scr.shape, F32)

    h = h_scr[...]
    act = _silu(_dot(h, wg_ref[...])) * _dot(h, wu_ref[...])
    acc_scr[...] += _dot(act.astype(BF16), wd_ref[...])

    @pl.when(f == pl.num_programs(1) - 1)
    def _():
        o_ref[...] = x_ref[...] + acc_scr[...]


def ffn_dense(x, g, wg, wu, wd, *, tm, tf):
    m, d = x.shape
    f = wg.shape[1]
    return pl.pallas_call(
        _ffn_kernel,
        out_shape=jax.ShapeDtypeStruct((m, d), F32),
        grid=(m // tm, f // tf),
        in_specs=[
            pl.BlockSpec((tm, d), lambda i, j: (i, 0)),
            pl.BlockSpec((1, d), lambda i, j: (0, 0)),
            pl.BlockSpec((d, tf), lambda i, j: (0, j)),
            pl.BlockSpec((d, tf), lambda i, j: (0, j)),
            pl.BlockSpec((tf, d), lambda i, j: (j, 0)),
        ],
        out_specs=pl.BlockSpec((tm, d), lambda i, j: (i, 0)),
        scratch_shapes=[pltpu.VMEM((tm, d), BF16), pltpu.VMEM((tm, d), F32)],
        compiler_params=_cparams(2),
        name="ffn_dense",
    )(x, g.reshape(1, d), wg, wu, wd)


def _router_kernel(x_ref, g_ref, wr_ref, br_ref, idx_ref, gate_ref):
    h = _rms(x_ref[...], g_ref[...]).astype(BF16)
    logits = _dot(h, wr_ref[...]) + br_ref[...]
    lane_i = lax.broadcasted_iota(jnp.int32, logits.shape, 1)
    lane = lane_i.astype(F32)
    big = float(ROUTER_LANES)
    t1 = jnp.max(logits, axis=-1, keepdims=True)
    i1 = jnp.min(jnp.where(logits == t1, lane, big), axis=-1, keepdims=True)
    rest = jnp.where(lane == i1, NEG_INF, logits)
    t2 = jnp.max(rest, axis=-1, keepdims=True)
    i2 = jnp.min(jnp.where(rest == t2, lane, big), axis=-1, keepdims=True)
    e2 = jnp.exp(t2 - t1)
    g1 = 1.0 / (1.0 + e2)
    idx_ref[...] = jnp.where(lane_i == 0, i1, i2).astype(jnp.int32)
    gate_ref[...] = jnp.where(lane_i == 0, g1, e2 * g1)


def moe_router(x, g, w_router_pad, b_router_pad, *, tm):
    m, d = x.shape
    wide = pl.BlockSpec((tm, ROUTER_LANES), lambda i: (i, 0))
    return pl.pallas_call(
        _router_kernel,
        out_shape=(jax.ShapeDtypeStruct((m, ROUTER_LANES), jnp.int32),
                   jax.ShapeDtypeStruct((m, ROUTER_LANES), F32)),
        grid=(m // tm,),
        in_specs=[
            pl.BlockSpec((tm, d), lambda i: (i, 0)),
            pl.BlockSpec((1, d), lambda i: (0, 0)),
            pl.BlockSpec((d, ROUTER_LANES), lambda i: (0, 0)),
            pl.BlockSpec((1, ROUTER_LANES), lambda i: (0, 0)),
        ],
        out_specs=(wide, wide),
        compiler_params=_cparams(1),
        name="moe_router",
    )(x, g.reshape(1, d), w_router_pad, b_router_pad)


def _moe_ffn_kernel(be_ref, nused_ref, tok_ref, tok_next_ref, x_hbm, g_ref, wg_ref, wu_ref, wd_ref, o_ref, xbuf,
                    h_scr, acc_scr, sem):
    b = pl.program_id(0)
    f = pl.program_id(1)
    n_used = nused_ref[0]
    bm = xbuf.shape[1]

    def row_copy(tok, r, slot):
        return pltpu.make_async_copy(x_hbm.at[pl.ds(tok[0, r], 1), :], xbuf.at[slot, pl.ds(r, 1), :], sem.at[slot])

    def start_rows(tok, slot):
        def body(i, carry):
            for u in range(ROW_DMA_UNROLL):
                row_copy(tok, i * ROW_DMA_UNROLL + u, slot).start()
            return carry
        lax.fori_loop(0, bm // ROW_DMA_UNROLL, body, 0)

    def wait_rows(tok, slot):
        pltpu.make_async_copy(x_hbm.at[pl.ds(0, bm), :], xbuf.at[slot], sem.at[slot]).wait()

    @pl.when(b < n_used)
    def _():
        @pl.when(f == 0)
        def _():
            slot = b % 2

            @pl.when(b == 0)
            def _():
                start_rows(tok_ref, 0)

            wait_rows(tok_ref, slot)

            @pl.when(b + 1 < n_used)
            def _():
                start_rows(tok_next_ref, 1 - slot)

            h_scr[...] = _rms(xbuf[slot], g_ref[...]).astype(BF16)
            acc_scr[...] = jnp.zeros(acc_scr.shape, F32)

        h = h_scr[...]
        act = _silu(_dot(h, wg_ref[...])) * _dot(h, wu_ref[...])
        acc_scr[...] += _dot(act.astype(BF16), wd_ref[...])

        @pl.when(f == pl.num_programs(1) - 1)
        def _():
            o_ref[...] = acc_scr[...].astype(o_ref.dtype)

    @pl.when(jnp.logical_and(b >= n_used, f == pl.num_programs(1) - 1))
    def _():
        o_ref[...] = jnp.zeros(o_ref.shape, o_ref.dtype)


def moe_ffn_grouped(block_expert, n_used, tok_buf, x, g, wg, wu, wd, *, bm, tf, out_dtype):
    l = tok_buf.shape[0]
    d = x.shape[1]
    n_blocks = l // bm
    n_f = wg.shape[2] // tf
    tok3 = tok_buf.reshape(n_blocks, 1, bm)

    def fidx(b, f, nused):
        return jnp.where(b < nused[0], f, n_f - 1)

    grid_spec = pltpu.PrefetchScalarGridSpec(
        num_scalar_prefetch=2,
        grid=(n_blocks, n_f),
        in_specs=[
            pl.BlockSpec((None, 1, bm), lambda b, f, be, nu: (b, 0, 0), memory_space=pltpu.SMEM),
            pl.BlockSpec((None, 1, bm), lambda b, f, be, nu: (jnp.minimum(b + 1, n_blocks - 1), 0, 0),
                         memory_space=pltpu.SMEM),
            pl.BlockSpec(memory_space=pl.ANY),
            pl.BlockSpec((1, d), lambda b, f, be, nu: (0, 0)),
            pl.BlockSpec((None, d, tf), lambda b, f, be, nu: (be[b], 0, fidx(b, f, nu))),
            pl.BlockSpec((None, d, tf), lambda b, f, be, nu: (be[b], 0, fidx(b, f, nu))),
            pl.BlockSpec((None, tf, d), lambda b, f, be, nu: (be[b], fidx(b, f, nu), 0)),
        ],
        out_specs=pl.BlockSpec((bm, d), lambda b, f, be, nu: (b, 0)),
        scratch_shapes=[
            pltpu.VMEM((2, bm, d), F32),
            pltpu.VMEM((bm, d), BF16),
            pltpu.VMEM((bm, d), F32),
            pltpu.SemaphoreType.DMA((2,)),
        ],
    )
    return pl.pallas_call(
        _moe_ffn_kernel,
        out_shape=jax.ShapeDtypeStruct((l, d), out_dtype),
        grid_spec=grid_spec,
        compiler_params=_cparams(2),
        name="moe_ffn_grouped",
    )(block_expert, n_used, tok3, tok3, x, g.reshape(1, d), wg, wu, wd)


def _combine_norm_kernel(x_ref, y0_ref, y1_ref, gate_ref, g_ref, xo_ref, yo_ref):
    gate = gate_ref[...]
    x = x_ref[...] + gate[:, 0:1] * y0_ref[...].astype(F32) + gate[:, 1:2] * y1_ref[...].astype(F32)
    xo_ref[...] = x
    yo_ref[...] = _rms(x, g_ref[...])


def combine_norm(x, y0, y1, gates, g_final, *, tm):
    m, d = x.shape
    blk = pl.BlockSpec((tm, d), lambda i: (i, 0))
    return pl.pallas_call(
        _combine_norm_kernel,
        out_shape=(jax.ShapeDtypeStruct((m, d), F32), jax.ShapeDtypeStruct((m, d), F32)),
        grid=(m // tm,),
        in_specs=[blk, blk, blk, pl.BlockSpec((tm, ROUTER_LANES), lambda i: (i, 0)),
                  pl.BlockSpec((1, d), lambda i: (0, 0))],
        out_specs=(blk, blk),
        compiler_params=_cparams(1),
        name="combine_norm",
    )(x, y0, y1, gates, g_final.reshape(1, d))


def moe_layer(x, g, w_router, b_router, wg, wu, wd, g_final, *, tm, bm, tf, y_dtype):
    m, d = x.shape
    wr = jnp.zeros((d, ROUTER_LANES), BF16).at[:, :N_EXPERTS].set(w_router.astype(BF16))
    br = jnp.full((1, ROUTER_LANES), NEG_INF, F32).at[0, :N_EXPERTS].set(b_router)
    idx, gates = moe_router(x, g, wr, br, tm=tm)

    nk = m * TOP_K
    flat_e = idx[:, :TOP_K].reshape(nk)
    onehot = (flat_e[:, None] == jnp.arange(N_EXPERTS, dtype=jnp.int32)[None, :]).astype(jnp.int32)
    csum = jnp.cumsum(onehot, axis=0)
    counts = csum[-1]
    rank = jnp.take_along_axis(csum, flat_e[:, None], axis=1)[:, 0] - 1
    padded = (counts + bm - 1) // bm * bm
    pad_end = jnp.cumsum(padded)
    pad_start = pad_end - padded
    dest = pad_start[flat_e] + rank
    l_rows = (nk + bm - 1) // bm * bm + N_EXPERTS * bm
    n_blocks = l_rows // bm
    tok_buf = jnp.zeros((l_rows,), jnp.int32).at[dest].set(jnp.arange(nk, dtype=jnp.int32) // TOP_K)
    block_expert = jnp.clip(
        jnp.searchsorted(pad_end, jnp.arange(n_blocks, dtype=jnp.int32) * bm, side="right"), 0, N_EXPERTS - 1
    ).astype(jnp.int32)
    n_used = (pad_end[-1] // bm).astype(jnp.int32).reshape(1)

    y_buf = moe_ffn_grouped(block_expert, n_used, tok_buf, x, g, wg, wu, wd, bm=bm, tf=tf, out_dtype=y_dtype)
    dest2 = dest.reshape(m, TOP_K)
    y0 = jnp.take(y_buf, dest2[:, 0], axis=0)
    y1 = jnp.take(y_buf, dest2[:, 1], axis=0)
    return combine_norm(x, y0, y1, gates, g_final, tm=tm)


def _sample_mix_weights(sgu_w, sgu_b, t_new, n_seq):
    r = jnp.arange(n_seq * t_new)
    t = r % t_new
    same = (r[:, None] // t_new) == (r[None, :] // t_new)
    causal = t[None, :] <= t[:, None]
    w = sgu_w[:, t[:, None], t[None, :]] * (same & causal)[None].astype(sgu_w.dtype)
    bias = jnp.repeat(sgu_b[:, t].T, CHUNK, axis=1)
    return w, bias


def kernel(x_prompt, x_sample, mem_prompt, cache_attn_k, cache_attn_v, cache_mem_k, cache_mem_v, page_table, norm_mix, w_in, b_gate, sgu_ln_g, sgu_ln_b, sgu_w, sgu_b, lam_q1, lam_k1, lam_q2, lam_k2, subln_g, mem_norm, w_mem_k, w_mem_v, w_br_a, w_br_b, w_br_m, w_out, norm_ffn, w_d_gate, w_d_up, w_d_down, w_router, b_router, w_e_gate, w_e_up, w_e_down, norm_final):
    batch, seq, d = x_prompt.shape
    n_seq, t_new, _ = x_sample.shape
    depth = w_in.shape[0]
    n_pages = page_table.shape[1]
    past_len = n_pages * PAGE_SIZE
    mp = batch * seq
    ms = n_seq * t_new
    assert ms == CHUNK and d == D_MODEL and N_QROWS == 2 * DA_HEADS * t_new

    xp = x_prompt.reshape(mp, d)
    xs = x_sample.reshape(ms, d)
    mem = mem_prompt.reshape(batch * N_MEM, d)
    slopes =jnp.asarray([2.0 ** (-8.0 * (i + 1) / DA_HEADS) for i in range(DA_HEADS)], F32)
    row_slopes = jnp.repeat(slopes, 2 * t_new).reshape(N_QROWS, 1)
    tril = jnp.tril(jnp.ones((CHUNK, CHUNK), F32))

    c_q = 2 * SGU_WIDTH
    c_k = c_q + DA_WIDTH
    c_qm = c_k + 2 * DA_WIDTH
    c_gate = c_qm + MEM_WIDTH

    outs = {k: [] for k in ("pk", "pv", "pmk", "pmv", "sk", "sv", "scv")}
    yp = ys = None
    for l in range(depth):
        lam_init = 0.8 - 0.6 * math.exp(-0.3 * l)
        lamp = jnp.stack([lam_q1[l], lam_k1[l], lam_q2[l], lam_k2[l]])
        w_l = w_in[l]
        w_z = jnp.concatenate([w_l[:, :c_k], w_l[:, c_qm:c_gate]], axis=1).astype(BF16)
        w_kv = w_l[:, c_k:c_qm].astype(BF16)
        w_g = w_l[:, c_gate:].astype(BF16)
        zero_bias = jnp.zeros((w_z.shape[1],), F32)
        w_mem = jnp.concatenate([w_mem_k[l], w_mem_v[l]], axis=1).astype(BF16)
        wa, wb, wm, wo = (w.astype(BF16) for w in (w_br_a[l], w_br_b[l], w_br_m[l], w_out[l]))
        w_mix_p = (sgu_w[l] * tril[None]).astype(BF16)
        bias_p = jnp.repeat(sgu_b[l].T, CHUNK, axis=1)
        w_mix_s, bias_s = _sample_mix_weights(sgu_w[l], sgu_b[l], t_new, n_seq)
        qm_blk = c_k // MEM_WIDTH

        mkv = norm_matmul(mem, mem_norm[l], w_mem, jnp.zeros((2 * MEM_WIDTH,), F32), sigmoid=False,
                          out_dtype=F32, tm=batch * N_MEM, tn=MEM_WIDTH)
        mk_p = mkv[:, :MEM_WIDTH].reshape(batch, N_MEM, MEM_WIDTH)
        mv_p = mkv[:, MEM_WIDTH:].reshape(batch, N_MEM, MEM_WIDTH)

        z = norm_matmul(xp, norm_mix[l], w_z, zero_bias, sigmoid=False, out_dtype=BF16, tm=1024, tn=512)
        kf, vf, kb, vt = norm_kv(xp, norm_mix[l], w_kv, tm=512, batch=batch)
        gates = norm_matmul(xp, norm_mix[l], w_g, b_gate[l], sigmoid=True, out_dtype=BF16, tm=1024, tn=512)
        (a,) = sgu(z, sgu_ln_g[l], sgu_ln_b[l], w_mix_p, bias_p, n_chunks=4, emit_v=False)
        bo = diff_attn_prompt(z, kb, vt, slopes, lamp, subln_g[l], batch=batch, seq=seq, lam_init=lam_init,
                              q_col0=c_q // DA_V_DIM, tq=1024)
        mo = mem_attn(z, mk_p, mv_p, n_batch=batch, rows_per_batch=seq, tq=512, q_col0=qm_blk)
        xp = merge(xp, a, bo, mo, gates, wa, wb, wm, wo, tm=512)

        zs = norm_matmul(xs, norm_mix[l], w_z, zero_bias, sigmoid=False, out_dtype=F32, tm=ms, tn=512)
        kfs, vfs, _, _ = norm_kv(xs, norm_mix[l], w_kv, tm=ms, batch=1)
        gates_s = norm_matmul(xs, norm_mix[l], w_g, b_gate[l], sigmoid=True, out_dtype=F32, tm=ms, tn=512)
        a_s, vrows = sgu(zs, sgu_ln_g[l], sgu_ln_b[l], w_mix_s.astype(BF16), bias_s, n_chunks=1, emit_v=True)
        q_s = zs[:, c_q:c_k].reshape(n_seq, 1, t_new, DA_WIDTH)
        q_rep = jnp.broadcast_to(q_s, (n_seq, 2 * DA_HEADS, t_new, DA_WIDTH)).reshape(n_seq, N_QROWS, DA_WIDTH)
        bo_s = diff_attn_sample(page_table, q_rep, kfs.reshape(n_seq, t_new, DA_WIDTH),
                                vfs.reshape(n_seq, t_new, DA_WIDTH), cache_attn_k, cache_attn_v, lamp, row_slopes,
                                subln_g[l], layer=l, past_len=past_len, lam_init=lam_init, pages_per_step=8)
        qm_s = zs[:, c_k:c_k + MEM_WIDTH].reshape(n_seq, t_new, MEM_WIDTH)
        qm_s = jnp.pad(qm_s, ((0, 0), (0, QM_PAD_ROWS - t_new), (0, 0))).reshape(n_seq * QM_PAD_ROWS, MEM_WIDTH)
        mo_s = mem_attn(qm_s, cache_mem_k[l].reshape(n_seq, N_MEM, MEM_WIDTH),
                        cache_mem_v[l].reshape(n_seq, N_MEM, MEM_WIDTH), n_batch=n_seq,
                        rows_per_batch=QM_PAD_ROWS, tq=QM_PAD_ROWS, q_col0=0)
        mo_s = mo_s.reshape(n_seq, QM_PAD_ROWS, MEM_WIDTH)[:, :t_new].reshape(ms, MEM_WIDTH)
        xs = merge(xs, a_s, bo_s.reshape(ms, DA_WIDTH), mo_s, gates_s, wa, wb, wm, wo, tm=ms)

        if l % 2 == 0:
            i = l // 2
            wg, wu, wd = w_d_gate[i].astype(BF16), w_d_up[i].astype(BF16), w_d_down[i].astype(BF16)
            xp = ffn_dense(xp, norm_ffn[l], wg, wu, wd, tm=512, tf=1408)
            xs = ffn_dense(xs, norm_ffn[l], wg, wu, wd, tm=ms, tf=1408)
        else:
            i = l // 2
            wg, wu, wd = w_e_gate[i].astype(BF16), w_e_up[i].astype(BF16), w_e_down[i].astype(BF16)
            xp, yp = moe_layer(xp, norm_ffn[l], w_router[i], b_router[i], wg, wu, wd, norm_final, tm=512, bm=512,
                               tf=1792, y_dtype=BF16)
            xs, ys = moe_layer(xs, norm_ffn[l], w_router[i], b_router[i], wg, wu, wd, norm_final, tm=ms, bm=128,
                               tf=1792, y_dtype=F32)

        outs["pk"].append(kf.reshape(batch, seq, DA_HEADS, DA_V_DIM))
        outs["pv"].append(vf.reshape(batch, seq, DA_HEADS, DA_V_DIM))
        outs["pmk"].append(mk_p.reshape(batch, N_MEM, MEM_HEADS, MEM_HEAD_DIM))
        outs["pmv"].append(mv_p.reshape(batch, N_MEM, MEM_HEADS, MEM_HEAD_DIM))
        outs["sk"].append(kfs.reshape(n_seq, t_new, DA_HEADS, DA_V_DIM))
        outs["sv"].append(vfs.reshape(n_seq, t_new, DA_HEADS, DA_V_DIM))
        outs["scv"].append(vrows.reshape(n_seq, t_new, SGU_WIDTH))

    if depth % 2 == 1:
        raise NotImplementedError("final norm is fused into the expert layer, which must come last")
    return (yp.reshape(batch, seq, d), ys.reshape(n_seq, t_new, d),
            jnp.stack(outs["pk"]), jnp.stack(outs["pv"]), jnp.stack(outs["pmk"]), jnp.stack(outs["pmv"]),
            jnp.stack(outs["sk"]), jnp.stack(outs["sv"]), jnp.stack(outs["scv"]))
```

```python
import functools
import math

import jax
import jax.numpy as jnp
from jax import lax
from jax.experimental import pallas as pl
from jax.experimental.pallas import tpu as pltpu

F32 = jnp.float32
BF16 = jnp.bfloat16

D_MODEL = 1024
CHUNK = 128
SGU_GROUPS = 4
SGU_WIDTH = 512
DA_HEADS = 8
DA_QK_DIM = 64
DA_V_DIM = 128
DA_WIDTH = DA_HEADS * DA_V_DIM
MEM_HEADS = 4
MEM_HEAD_DIM = 128
MEM_WIDTH = MEM_HEADS * MEM_HEAD_DIM
N_MEM = 256
N_EXPERTS = 8
TOP_K = 2
PAGE_SIZE = 128
EPS = 1e-5
NEG_INF = float("-inf")

VMEM_LIMIT_BYTES = 48 * 1024 * 1024
LANES = 128
ROUTER_LANES = 128
QM_PAD_ROWS = 16
ALIBI_SPLIT = 16
SUM_ROWS = 16
LOG2E = 1.4426950408889634
ROW_DMA_UNROLL = 8


def _cparams(n_axes):
    return pltpu.CompilerParams(
        dimension_semantics=("arbitrary",) * n_axes, vmem_limit_bytes=VMEM_LIMIT_BYTES)


def _rms(x, g):
    ms = jnp.mean(x * x, axis=-1, keepdims=True)
    return x * lax.rsqrt(ms + EPS) * g


def _gelu(x):
    return 0.5 * x * (1.0 + lax.erf(x * (1.0 / math.sqrt(2.0))))


def _sigmoid(x):
    return 0.5 * (jnp.tanh(0.5 * x) + 1.0)


def _dot(a, b):
    return jnp.dot(a, b, preferred_element_type=F32)


def _dot_nt(a, b):
    return lax.dot_general(a, b, (((1,), (1,)), ((), ())), preferred_element_type=F32)


def _norm_matmul_kernel(x_ref, g_ref, w_ref, b_ref, o_ref, h_scr, *, sigmoid):
    @pl.when(pl.program_id(1) == 0)
    def _():
        h_scr[...] = _rms(x_ref[...], g_ref[...]).astype(BF16)

    acc = _dot(h_scr[...], w_ref[...])
    if sigmoid:
        acc = _sigmoid(acc + b_ref[...])
    o_ref[...] = acc.astype(o_ref.dtype)


def norm_matmul(x, g, w, bias, *, sigmoid, out_dtype, tm, tn):
    m, d = x.shape
    n = w.shape[1]
    return pl.pallas_call(
        functools.partial(_norm_matmul_kernel, sigmoid=sigmoid),
        out_shape=jax.ShapeDtypeStruct((m, n), out_dtype),
        grid=(m // tm, n // tn),
        in_specs=[
            pl.BlockSpec((tm, d), lambda i, j: (i, 0)),
            pl.BlockSpec((1, d), lambda i, j: (0, 0)),
            pl.BlockSpec((d, tn), lambda i, j: (0, j)),
            pl.BlockSpec((1, tn), lambda i, j: (0, j)),
        ],
        out_specs=pl.BlockSpec((tm, tn), lambda i, j: (i, j)),
        scratch_shapes=[pltpu.VMEM((tm, d), BF16)],
        compiler_params=_cparams(2),
        name="norm_matmul_sig" if sigmoid else "norm_matmul",
    )(x, g.reshape(1, d), w, bias.reshape(1, n))


def _norm_kv_kernel(x_ref, g_ref, w_ref, kf_ref, vf_ref, kb_ref, vt_ref, h_scr):
    j = pl.program_id(1)

    @pl.when(j == 0)
    def _():
        h_scr[...] = _rms(x_ref[...], g_ref[...]).astype(BF16)

    acc = _dot(h_scr[...], w_ref[...])

    @pl.when(j == 0)
    def _():
        kf_ref[...] = acc
        kb_ref[...] = acc.astype(BF16)

    @pl.when(j == 1)
    def _():
        vf_ref[...] = acc
        vt_ref[...] = acc.T.astype(BF16)


def norm_kv(x, g, w_kv, *, tm, batch):
    m, d = x.shape
    n = w_kv.shape[1] // 2
    nb = m // batch // tm
    blk = pl.BlockSpec((tm, n), lambda i, j: (i, 0))
    return pl.pallas_call(
        _norm_kv_kernel,
        out_shape=(jax.ShapeDtypeStruct((m, n), F32), jax.ShapeDtypeStruct((m, n), F32),
                   jax.ShapeDtypeStruct((m, n), BF16), jax.ShapeDtypeStruct((batch, n, m // batch), BF16)),
        grid=(m // tm, 2),
        in_specs=[
            pl.BlockSpec((tm, d), lambda i, j: (i, 0)),
            pl.BlockSpec((1, d), lambda i, j: (0, 0)),
            pl.BlockSpec((d, n), lambda i, j: (0, j)),
        ],
        out_specs=(blk, blk, blk, pl.BlockSpec((None, n, tm), lambda i, j: (i // nb, 0, i % nb))),
        scratch_shapes=[pltpu.VMEM((tm, d), BF16)],
        compiler_params=_cparams(2),
        name="norm_kv",
    )(x, g.reshape(1, d), w_kv)


def _sgu_kernel(u_ref, v_ref, lng_ref, lnb_ref, w_ref, bias_ref, a_ref, *maybe_vn_ref, n_chunks):
    u = _gelu(u_ref[...].astype(F32))
    v = _gelu(v_ref[...].astype(F32))
    mu = jnp.mean(v, axis=-1, keepdims=True)
    vc = v - mu
    var = jnp.mean(vc * vc, axis=-1, keepdims=True)
    vn = vc * lax.rsqrt(var + EPS) * lng_ref[...] + lnb_ref[...]
    if maybe_vn_ref:
        maybe_vn_ref[0][...] = vn
    vnb = vn.astype(BF16)
    bias = bias_ref[...]
    for c in range(n_chunks):
        rows = slice(c * CHUNK, (c + 1) * CHUNK)
        for g in range(SGU_GROUPS):
            cols = slice(g * CHUNK, (g + 1) * CHUNK)
            mixed = _dot(w_ref[g], vnb[rows, cols]) + bias[:, cols]
            a_ref[rows, cols] = (u[rows, cols] * mixed).astype(a_ref.dtype)


def sgu(z, ln_g, ln_b, w_mix, bias_full, *, n_chunks, emit_v):
    m = z.shape[0]
    rows = n_chunks * CHUNK
    row_blk = pl.BlockSpec((rows, SGU_WIDTH), lambda i: (i, 0))
    out_shape = [jax.ShapeDtypeStruct((m, SGU_WIDTH), BF16)]
    out_specs = [row_blk]
    if emit_v:
        out_shape.append(jax.ShapeDtypeStruct((m, SGU_WIDTH), F32))
        out_specs.append(row_blk)
    return pl.pallas_call(
        functools.partial(_sgu_kernel, n_chunks=n_chunks),
        out_shape=tuple(out_shape),
        grid=(m // rows,),
        in_specs=[
            row_blk,
            pl.BlockSpec((rows, SGU_WIDTH), lambda i: (i, 1)),
            pl.BlockSpec((1, SGU_WIDTH), lambda i: (0, 0)),
            pl.BlockSpec((1, SGU_WIDTH), lambda i: (0, 0)),
            pl.BlockSpec((SGU_GROUPS, CHUNK, CHUNK), lambda i: (0, 0, 0)),
            pl.BlockSpec((CHUNK, SGU_WIDTH), lambda i: (0, 0)),
        ],
        out_specs=tuple(out_specs),
        compiler_params=_cparams(1),
        name="sgu",
    )(z, z, ln_g.reshape(1, -1), ln_b.reshape(1, -1), w_mix, bias_full)


def _diff_lambda(lamp, lam_init):
    s1 = jnp.sum(lamp[0:1, :] * lamp[1:2, :], axis=-1, keepdims=True)
    s2 = jnp.sum(lamp[2:3, :] * lamp[3:4, :], axis=-1, keepdims=True)
    return jnp.exp(s1) - jnp.exp(s2) + lam_init


def _sub_norm(o, g, lam_init):
    return _rms(o, g) * (1.0 - lam_init)


def _diff_attn_kernel(slopes_ref, lamp_ref, q_ref, k_ref, vt_ref, g_ref, o_ref, causal_scr, q_alibi_scr,
                      k_alibi_scr, *, tq, lam_init):
    h = pl.program_id(1)
    qi = pl.program_id(2)
    slope2 = slopes_ref[h] * LOG2E

    @pl.when((pl.program_id(0) == 0) & (h == 0) & (qi == 0))
    def _():
        krow = lax.broadcasted_iota(jnp.int32, (tq, tq), 0)
        qcol = lax.broadcasted_iota(jnp.int32, (tq, tq), 1)
        causal_scr[...] = jnp.where(krow <= qcol, 0.0, NEG_INF)

    q = (q_ref[...].astype(F32) * (DA_QK_DIM ** -0.5 * LOG2E)).astype(BF16)
    lane = lax.broadcasted_iota(jnp.int32, (tq, LANES), 1)

    @pl.when(qi == 0)
    def _():
        pos = lax.broadcasted_iota(jnp.int32, (tq, LANES), 0)
        hi = (pos // ALIBI_SPLIT).astype(F32)
        lo = (pos % ALIBI_SPLIT).astype(F32)
        c_hi = slope2.astype(BF16).astype(F32)
        c_lo = (slope2 - c_hi).astype(BF16).astype(F32)
        qa = jnp.zeros((tq, LANES), F32)
        ka = jnp.zeros((tq, LANES), F32)
        for n, c in enumerate((c_hi, c_lo)):
            qa = jnp.where(lane == 4 * n, c * ALIBI_SPLIT, jnp.where(lane == 4 * n + 1, c, jnp.where(
                lane == 4 * n + 2, -c * ALIBI_SPLIT * hi, jnp.where(lane == 4 * n + 3, -c * lo, qa))))
            ka = jnp.where(lane == 4 * n, hi, jnp.where(lane == 4 * n + 1, lo, jnp.where(
                (lane == 4 * n + 2) | (lane == 4 * n + 3), 1.0, ka)))
        q_alibi_scr[...] = qa.astype(BF16)
        k_alibi_scr[...] = ka.astype(BF16)

    q_alibi = q_alibi_scr[...]
    k_alibi = k_alibi_scr[...]
    qzero = jnp.zeros_like(q)
    qz = (jnp.concatenate([jnp.where(lane < DA_QK_DIM, q, qzero), q_alibi], axis=1),
          jnp.concatenate([jnp.where(lane >= DA_QK_DIM, q, qzero), q_alibi], axis=1))
    ones = jnp.ones((SUM_ROWS, tq), BF16)

    def update(carry, start, shift, k0=0, nk=tq, c0=0, masked=False):
        rows = pl.ds(pl.multiple_of(start + k0, nk), nk)
        kt = jnp.concatenate([k_ref[rows, :], k_alibi[k0:k0 + nk]], axis=1)
        vt = jnp.concatenate([vt_ref[:, rows], ones[:, :nk]], axis=0)
        ts = [_dot_nt(kt, qm[c0:]) for qm in qz]
        new = []
        for (m, acc), t in zip(carry, ts):
            if masked:
                t = t + causal_scr[k0:k0 + nk, c0:]
            m_new = jnp.maximum(m[:, c0:], jnp.max(t, axis=0, keepdims=True) + shift)
            p = jnp.exp2(t - (m_new - shift))
            acc_new = jnp.exp2(m[:, c0:] - m_new) * acc[:, c0:] + _dot(vt, p.astype(BF16))
            if c0:
                m_new = jnp.concatenate([m[:, :c0], m_new], axis=1)
                acc_new = jnp.concatenate([acc[:, :c0], acc_new], axis=1)
            new.append((m_new, acc_new))
        return tuple(new)

    def body(j, carry):
        shift = ((j - qi) * tq).astype(F32) * slope2
        return update(carry, j * tq, shift)

    one = (jnp.full((1, tq), NEG_INF, F32), jnp.zeros((DA_V_DIM + SUM_ROWS, tq), F32))
    carry = lax.fori_loop(0, qi, body, (one, one))
    half = tq // 2
    carry = update(carry, qi * tq, 0.0, k0=0, nk=half, c0=0, masked=True)
    (_, a1), (_, a2) = update(carry, qi * tq, 0.0, k0=half, nk=half, c0=half, masked=True)
    lam = _diff_lambda(lamp_ref[...], lam_init)
    o1 = a1[:DA_V_DIM] / a1[DA_V_DIM:DA_V_DIM + 1]
    o2 = a2[:DA_V_DIM] / a2[DA_V_DIM:DA_V_DIM + 1]
    o = (o1 - lam * o2).T
    o_ref[...] = _sub_norm(o, g_ref[...], lam_init).astype(o_ref.dtype)


def diff_attn_prompt(z, kb, vt, slopes, lamp, subln_g, *, batch, seq, lam_init, q_col0, tq):
    nq = seq // tq
    grid_spec = pltpu.PrefetchScalarGridSpec(
        num_scalar_prefetch=1,
        grid=(batch, DA_HEADS, nq),
        in_specs=[
            pl.BlockSpec((4, DA_QK_DIM), lambda b, h, i, s: (0, 0)),
            pl.BlockSpec((tq, DA_V_DIM), lambda b, h, i, s: (b * nq + i, q_col0 + h)),
            pl.BlockSpec((seq, DA_V_DIM), lambda b, h, i, s: (b, h)),
            pl.BlockSpec((None, DA_V_DIM, seq), lambda b, h, i, s: (b, h, 0)),
            pl.BlockSpec((1, DA_V_DIM), lambda b, h, i, s: (0, 0)),
        ],
        out_specs=pl.BlockSpec((tq, DA_V_DIM), lambda b, h, i, s: (b * nq + i, h)),
        scratch_shapes=[pltpu.VMEM((tq, tq), F32),
                        pltpu.VMEM((tq, LANES), BF16), pltpu.VMEM((tq, LANES), BF16)],
    )
    return pl.pallas_call(
        functools.partial(_diff_attn_kernel, tq=tq, lam_init=lam_init),
        out_shape=jax.ShapeDtypeStruct((batch * seq, DA_WIDTH), BF16),
        grid_spec=grid_spec,
        compiler_params=_cparams(3),
        name="diff_attn_prompt",
    )(slopes, lamp, z, kb, vt, subln_g.reshape(1, -1))


N_QROWS = DA_HEADS * 2 * 4


def _paged_attn_kernel(pt_ref, lamp_ref, slope_ref, q_ref, kn_ref, vn_ref, g_ref, ck_hbm, cv_hbm, o_ref, kv_buf,
                       qz_scr, s_scr, sn_scr, m_scr, l_scr, acc_scr, sem, *, pages_per_step, t_new, past_len,
                       lam_init, layer):
    pp = pages_per_step
    b = pl.program_id(0)
    phase = pl.program_id(1)
    g = pl.program_id(2)
    n_seq = pl.num_programs(0)
    n_groups = pl.num_programs(2)
    last = n_groups - 1
    step_keys = pp * PAGE_SIZE
    rows_per_head = 2 * t_new
    slot = (phase * n_groups + g) % 2

    def tile_copy(cache_hbm, seq, grp, r, h, dst_slot):
        page_id = pt_ref[seq, grp * pp + r]
        return pltpu.make_async_copy(cache_hbm.at[layer, page_id, :, h, :], kv_buf.at[dst_slot, r, h],
                                     sem.at[dst_slot])

    def start_step(seq, ph, grp, dst_slot):
        for cache_hbm, which in ((ck_hbm, 0), (cv_hbm, 1)):
            @pl.when(ph == which)
            def _():
                for r in range(pp):
                    for h in range(DA_HEADS):
                        tile_copy(cache_hbm, seq, grp, r, h, dst_slot).start()

    def wait_step():
        for cache_hbm, which in ((ck_hbm, 0), (cv_hbm, 1)):
            @pl.when(phase == which)
            def _():
                for r in range(pp):
                    for h in range(DA_HEADS):
                        tile_copy(cache_hbm, b, g, r, h, slot).wait()

    @pl.when((b == 0) & (phase == 0) & (g == 0))
    def _():
        start_step(b, phase, g, slot)

    wrap = g == last
    nxt_g = jnp.where(wrap, 0, g + 1)
    nxt_phase = jnp.where(wrap, 1 - phase, phase)
    nxt_b = jnp.where(wrap & (phase == 1), b + 1, b)

    @pl.when(nxt_b < n_seq)
    def _():
        start_step(nxt_b, nxt_phase, nxt_g, 1 - slot)

    wait_step()

    rowi = lax.broadcasted_iota(jnp.int32, (N_QROWS, 1), 0)
    slope = slope_ref[...]
    tok = rowi % t_new
    lam = _diff_lambda(lamp_ref[...], lam_init)

    def page(r):
        return jnp.concatenate([kv_buf[slot, r, h].astype(BF16) for h in range(DA_HEADS)], axis=1)

    def combine(p):
        return pltpu.roll(p, t_new, axis=0) - lam * p

    def new_key_scores():
        kn = jnp.concatenate([kn_ref[...].astype(BF16), jnp.zeros((PAGE_SIZE - t_new, DA_WIDTH), BF16)], axis=0)
        s_all = _dot_nt(qz_scr[...], kn)
        row = lax.broadcasted_iota(jnp.int32, s_all.shape, 0)
        j = lax.broadcasted_iota(jnp.int32, s_all.shape, 1)
        valid = (j < t_new) & (j <= row % t_new)
        return jnp.where(valid, s_all - slope * (row % t_new - j).astype(F32), NEG_INF)

    @pl.when(jnp.logical_and(phase == 0, g == 0))
    def _():
        row = lax.broadcasted_iota(jnp.int32, (N_QROWS, DA_WIDTH), 0)
        lane = lax.broadcasted_iota(jnp.int32, (N_QROWS, DA_WIDTH), 1)
        keep = (lane // DA_QK_DIM) == (row // t_new)
        qz_scr[...] = jnp.where(keep, q_ref[...] * (DA_QK_DIM ** -0.5), 0.0).astype(BF16)
        m_scr[...] = jnp.full(m_scr.shape, NEG_INF, F32)
        l_scr[...] = jnp.zeros(l_scr.shape, F32)
        acc_scr[...] = jnp.zeros(acc_scr.shape, F32)

    def online(s):
        m_old = m_scr[...]
        m_new = jnp.maximum(m_old, jnp.max(s, axis=-1, keepdims=True))
        l_scr[...] = jnp.exp(m_old - m_new) * l_scr[...] + jnp.sum(jnp.exp(s - m_new), axis=-1, keepdims=True)
        m_scr[...] = m_new

    @pl.when(phase == 0)
    def _():
        qz = qz_scr[...]
        s = jnp.concatenate([_dot_nt(qz, page(r)) for r in range(pp)], axis=1)
        kpos = g * step_keys + lax.broadcasted_iota(jnp.int32, (1, step_keys), 1)
        s = s - slope * ((past_len + tok) - kpos).astype(F32)
        s_scr[:, pl.ds(pl.multiple_of(g * step_keys, step_keys), step_keys)] = s
        online(s)

        @pl.when(g == last)
        def _():
            sn = new_key_scores()
            sn_scr[...] = sn
            online(sn)

    @pl.when(phase == 1)
    def _():
        s = s_scr[:, pl.ds(pl.multiple_of(g * step_keys, step_keys), step_keys)]
        m = m_scr[...]
        inv_l = 1.0 / l_scr[...]
        pc = combine(jnp.exp(s - m) * inv_l).astype(BF16)
        pv = _dot(pc[:, 0:PAGE_SIZE], page(0))
        for r in range(1, pp):
            pv += _dot(pc[:, r * PAGE_SIZE:(r + 1) * PAGE_SIZE], page(r))
        acc_scr[...] += pv

        @pl.when(g == last)
        def _():
            pn = combine(jnp.exp(sn_scr[...] - m) * inv_l).astype(BF16)
            vn = jnp.concatenate([vn_ref[...].astype(BF16), jnp.zeros((PAGE_SIZE - t_new, DA_WIDTH), BF16)], axis=0)
            acc = acc_scr[...] + _dot(pn, vn)
            outs = []
            for h in range(DA_HEADS):
                r0 = h * rows_per_head + t_new
                outs.append(_sub_norm(acc[r0:r0 + t_new, h * DA_V_DIM:(h + 1) * DA_V_DIM], g_ref[...], lam_init))
            o_ref[...] = jnp.concatenate(outs, axis=1)


def diff_attn_sample(page_table, q_rep, k_new, v_new, cache_k, cache_v, lamp, row_slopes, subln_g, *, layer,
                     past_len, lam_init, pages_per_step):
    n_seq, t_new = k_new.shape[:2]
    n_pages = page_table.shape[1]
    pp = pages_per_step
    n_groups = n_pages // pp
    assert n_pages % pp == 0
    new_spec = pl.BlockSpec((None, t_new, DA_WIDTH), lambda b, ph, g, pt: (b, 0, 0))
    grid_spec = pltpu.PrefetchScalarGridSpec(
        num_scalar_prefetch=1,
        grid=(n_seq, 2, n_groups),
        in_specs=[
            pl.BlockSpec((4, DA_QK_DIM), lambda b, ph, g, pt: (0, 0)),
            pl.BlockSpec((N_QROWS, 1), lambda b, ph, g, pt: (0, 0)),
            pl.BlockSpec((None, N_QROWS, DA_WIDTH), lambda b, ph, g, pt: (b, 0, 0)),
            new_spec, new_spec,
            pl.BlockSpec((1, DA_V_DIM), lambda b, ph, g, pt: (0, 0)),
            pl.BlockSpec(memory_space=pl.ANY),
            pl.BlockSpec(memory_space=pl.ANY),
        ],
        out_specs=pl.BlockSpec((None, t_new, DA_WIDTH), lambda b, ph, g, pt: (b, 0, 0)),
        scratch_shapes=[
            pltpu.VMEM((2, pp, DA_HEADS, PAGE_SIZE, DA_V_DIM), F32),
            pltpu.VMEM((N_QROWS, DA_WIDTH), BF16),
            pltpu.VMEM((N_QROWS, n_pages * PAGE_SIZE), F32),
            pltpu.VMEM((N_QROWS, PAGE_SIZE), F32),
            pltpu.VMEM((N_QROWS, 1), F32),
            pltpu.VMEM((N_QROWS, 1), F32),
            pltpu.VMEM((N_QROWS, DA_WIDTH), F32),
            pltpu.SemaphoreType.DMA((2,)),
        ],
    )
    return pl.pallas_call(
        functools.partial(_paged_attn_kernel, pages_per_step=pp, t_new=t_new, past_len=past_len, lam_init=lam_init,
                          layer=layer),
        out_shape=jax.ShapeDtypeStruct((n_seq, t_new, DA_WIDTH), F32),
        grid_spec=grid_spec,
        compiler_params=_cparams(3),
        name="diff_attn_sample",
    )(page_table, lamp, row_slopes, q_rep, k_new, v_new, subln_g.reshape(1, -1), cache_k, cache_v)


def _mem_attn_kernel(q_ref, mk_ref, mv_ref, o_ref):
    scale = MEM_HEAD_DIM ** -0.5
    q = q_ref[...].astype(BF16)
    mk = mk_ref[...].astype(BF16)
    mv = mv_ref[...].astype(BF16)
    outs = []
    for h in range(MEM_HEADS):
        cols = slice(h * MEM_HEAD_DIM, (h + 1) * MEM_HEAD_DIM)
        s = _dot_nt(q[:, cols], mk[:, cols]) * scale
        e = jnp.exp(s - jnp.max(s, axis=-1, keepdims=True))
        p = e / jnp.sum(e, axis=-1, keepdims=True)
        outs.append(_dot(p.astype(BF16), mv[:, cols]))
    o_ref[...] = jnp.concatenate(outs, axis=1).astype(o_ref.dtype)


def mem_attn(q_arr, mk, mv, *, n_batch, rows_per_batch, tq, q_col0):
    nq = rows_per_batch // tq
    return pl.pallas_call(
        _mem_attn_kernel,
        out_shape=jax.ShapeDtypeStruct((n_batch * rows_per_batch, MEM_WIDTH), BF16),
        grid=(n_batch, nq),
        in_specs=[
            pl.BlockSpec((tq, MEM_WIDTH), lambda b, i: (b * nq + i, q_col0)),
            pl.BlockSpec((None, N_MEM, MEM_WIDTH), lambda b, i: (b, 0, 0)),
            pl.BlockSpec((None, N_MEM, MEM_WIDTH), lambda b, i: (b, 0, 0)),
        ],
        out_specs=pl.BlockSpec((tq, MEM_WIDTH), lambda b, i: (b * nq + i, 0)),
        compiler_params=_cparams(2),
        name="mem_attn",
    )(q_arr, mk, mv)


def _merge_kernel(x_ref, a_ref, b_ref, m_ref, gate_ref, wa_ref, wb_ref, wm_ref, wo_ref, o_ref):
    d = D_MODEL
    ga = gate_ref[:, 0:d]
    gb = gate_ref[:, d:2 * d]
    gm = gate_ref[:, 2 * d:3 * d]
    y = ga.astype(F32) * _dot(a_ref[...].astype(BF16), wa_ref[...])
    y += gb.astype(F32) * _dot(b_ref[...].astype(BF16), wb_ref[...])
    y += gm.astype(F32) * _dot(m_ref[...].astype(BF16), wm_ref[...])
    o_ref[...] = x_ref[...] + _dot(y.astype(BF16), wo_ref[...])


def merge(x, a, b, mo, gates, wa, wb, wm, wo, *, tm):
    m, d = x.shape
    full = lambda arr: pl.BlockSpec(arr.shape, lambda i: (0, 0))
    rowblk = lambda arr: pl.BlockSpec((tm, arr.shape[1]), lambda i: (i, 0))
    return pl.pallas_call(
        _merge_kernel,
        out_shape=jax.ShapeDtypeStruct((m, d), F32),
        grid=(m // tm,),
        in_specs=[rowblk(x), rowblk(a), rowblk(b), rowblk(mo), rowblk(gates), full(wa), full(wb), full(wm), full(wo)],
        out_specs=pl.BlockSpec((tm, d), lambda i: (i, 0)),
        compiler_params=_cparams(1),
        name="merge",
    )(x, a, b, mo, gates, wa, wb, wm, wo)


def _silu(x):
    return x * _sigmoid(x)


def _ffn_kernel(x_ref, g_ref, wg_ref, wu_ref, wd_ref, o_ref, h_scr, acc_scr):
    f = pl.program_id(1)

    @pl.when(f == 0)
    def _():
        h_scr[...] = _rms(x_ref[...], g_ref[...]).astype(BF16)
        acc_scr[...] = jnp.zeros(acc_scr.shape, F32)

    h = h_scr[...]
    act = _silu(_dot(h, wg_ref[...])) * _dot(h, wu_ref[...])
    acc_scr[...] += _dot(act.astype(BF16), wd_ref[...])

    @pl.when(f == pl.num_programs(1) - 1)
    def _():
        o_ref[...] = x_ref[...] + acc_scr[...]


def ffn_dense(x, g, wg, wu, wd, *, tm, tf):
    m, d = x.shape
    f = wg.shape[1]
    return pl.pallas_call(
        _ffn_kernel,
        out_shape=jax.ShapeDtypeStruct((m, d), F32),
        grid=(m // tm, f // tf),
        in_specs=[
            pl.BlockSpec((tm, d), lambda i, j: (i, 0)),
            pl.BlockSpec((1, d), lambda i, j: (0, 0)),
            pl.BlockSpec((d, tf), lambda i, j: (0, j)),
            pl.BlockSpec((d, tf), lambda i, j: (0, j)),
            pl.BlockSpec((tf, d), lambda i, j: (j, 0)),
        ],
        out_specs=pl.BlockSpec((tm, d), lambda i, j: (i, 0)),
        scratch_shapes=[pltpu.VMEM((tm, d), BF16), pltpu.VMEM((tm, d), F32)],
        compiler_params=_cparams(2),
        name="ffn_dense",
    )(x, g.reshape(1, d), wg, wu, wd)


def _router_kernel(x_ref, g_ref, wr_ref, br_ref, idx_ref, gate_ref):
    h = _rms(x_ref[...], g_ref[...]).astype(BF16)
    logits = _dot(h, wr_ref[...]) + br_ref[...]
    lane_i = lax.broadcasted_iota(jnp.int32, logits.shape, 1)
    lane = lane_i.astype(F32)
    big = float(ROUTER_LANES)
    t1 = jnp.max(logits, axis=-1, keepdims=True)
    i1 = jnp.min(jnp.where(logits == t1, lane, big), axis=-1, keepdims=True)
    rest = jnp.where(lane == i1, NEG_INF, logits)
    t2 = jnp.max(rest, axis=-1, keepdims=True)
    i2 = jnp.min(jnp.where(rest == t2, lane, big), axis=-1, keepdims=True)
    e2 = jnp.exp(t2 - t1)
    g1 = 1.0 / (1.0 + e2)
    idx_ref[...] = jnp.where(lane_i == 0, i1, i2).astype(jnp.int32)
    gate_ref[...] = jnp.where(lane_i == 0, g1, e2 * g1)


def moe_router(x, g, w_router_pad, b_router_pad, *, tm):
    m, d = x.shape
    wide = pl.BlockSpec((tm, ROUTER_LANES), lambda i: (i, 0))
    return pl.pallas_call(
        _router_kernel,
        out_shape=(jax.ShapeDtypeStruct((m, ROUTER_LANES), jnp.int32),
                   jax.ShapeDtypeStruct((m, ROUTER_LANES), F32)),
        grid=(m // tm,),
        in_specs=[
            pl.BlockSpec((tm, d), lambda i: (i, 0)),
            pl.BlockSpec((1, d), lambda i: (0, 0)),
            pl.BlockSpec((d, ROUTER_LANES), lambda i: (0, 0)),
            pl.BlockSpec((1, ROUTER_LANES), lambda i: (0, 0)),
        ],
        out_specs=(wide, wide),
        compiler_params=_cparams(1),
        name="moe_router",
    )(x, g.reshape(1, d), w_router_pad, b_router_pad)


def _moe_ffn_kernel(be_ref, nused_ref, tok_ref, tok_next_ref, x_hbm, g_ref, wg_ref, wu_ref, wd_ref, o_ref, xbuf,
                    h_scr, acc_scr, sem):
    b = pl.program_id(0)
    f = pl.program_id(1)
    n_used = nused_ref[0]
    bm = xbuf.shape[1]

    def row_copy(tok, r, slot):
        return pltpu.make_async_copy(x_hbm.at[pl.ds(tok[0, r], 1), :], xbuf.at[slot, pl.ds(r, 1), :], sem.at[slot])

    def start_rows(tok, slot):
        def body(i, carry):
            for u in range(ROW_DMA_UNROLL):
                row_copy(tok, i * ROW_DMA_UNROLL + u, slot).start()
            return carry
        lax.fori_loop(0, bm // ROW_DMA_UNROLL, body, 0)

    def wait_rows(tok, slot):
        pltpu.make_async_copy(x_hbm.at[pl.ds(0, bm), :], xbuf.at[slot], sem.at[slot]).wait()

    @pl.when(b < n_used)
    def _():
        @pl.when(f == 0)
        def _():
            slot = b % 2

            @pl.when(b == 0)
            def _():
                start_rows(tok_ref, 0)

            wait_rows(tok_ref, slot)

            @pl.when(b + 1 < n_used)
            def _():
                start_rows(tok_next_ref, 1 - slot)

            h_scr[...] = _rms(xbuf[slot], g_ref[...]).astype(BF16)
            acc_scr[...] = jnp.zeros(acc_scr.shape, F32)

        h = h_scr[...]
        act = _silu(_dot(h, wg_ref[...])) * _dot(h, wu_ref[...])
        acc_scr[...] += _dot(act.astype(BF16), wd_ref[...])

        @pl.when(f == pl.num_programs(1) - 1)
        def _():
            o_ref[...] = acc_scr[...].astype(o_ref.dtype)

    @pl.when(jnp.logical_and(b >= n_used, f == pl.num_programs(1) - 1))
    def _():
        o_ref[...] = jnp.zeros(o_ref.shape, o_ref.dtype)


def moe_ffn_grouped(block_expert, n_used, tok_buf, x, g, wg, wu, wd, *, bm, tf, out_dtype):
    l = tok_buf.shape[0]
    d = x.shape[1]
    n_blocks = l // bm
    n_f = wg.shape[2] // tf
    tok3 = tok_buf.reshape(n_blocks, 1, bm)

    def fidx(b, f, nused):
        return jnp.where(b < nused[0], f, n_f - 1)

    grid_spec = pltpu.PrefetchScalarGridSpec(
        num_scalar_prefetch=2,
        grid=(n_blocks, n_f),
        in_specs=[
            pl.BlockSpec((None, 1, bm), lambda b, f, be, nu: (b, 0, 0), memory_space=pltpu.SMEM),
            pl.BlockSpec((None, 1, bm), lambda b, f, be, nu: (jnp.minimum(b + 1, n_blocks - 1), 0, 0),
                         memory_space=pltpu.SMEM),
            pl.BlockSpec(memory_space=pl.ANY),
            pl.BlockSpec((1, d), lambda b, f, be, nu: (0, 0)),
            pl.BlockSpec((None, d, tf), lambda b, f, be, nu: (be[b], 0, fidx(b, f, nu))),
            pl.BlockSpec((None, d, tf), lambda b, f, be, nu: (be[b], 0, fidx(b, f, nu))),
            pl.BlockSpec((None, tf, d), lambda b, f, be, nu: (be[b], fidx(b, f, nu), 0)),
        ],
        out_specs=pl.BlockSpec((bm, d), lambda b, f, be, nu: (b, 0)),
        scratch_shapes=[
            pltpu.VMEM((2, bm, d), F32),
            pltpu.VMEM((bm, d), BF16),
            pltpu.VMEM((bm, d), F32),
            pltpu.SemaphoreType.DMA((2,)),
        ],
    )
    return pl.pallas_call(
        _moe_ffn_kernel,
        out_shape=jax.ShapeDtypeStruct((l, d), out_dtype),
        grid_spec=grid_spec,
        compiler_params=_cparams(2),
        name="moe_ffn_grouped",
    )(block_expert, n_used, tok3, tok3, x, g.reshape(1, d), wg, wu, wd)


def _combine_norm_kernel(x_ref, y0_ref, y1_ref, gate_ref, g_ref, xo_ref, yo_ref):
    gate = gate_ref[...]
    x = x_ref[...] + gate[:, 0:1] * y0_ref[...].astype(F32) + gate[:, 1:2] * y1_ref[...].astype(F32)
    xo_ref[...] = x
    yo_ref[...] = _rms(x, g_ref[...])


def combine_norm(x, y0, y1, gates, g_final, *, tm):
    m, d = x.shape
    blk = pl.BlockSpec((tm, d), lambda i: (i, 0))
    return pl.pallas_call(
        _combine_norm_kernel,
        out_shape=(jax.ShapeDtypeStruct((m, d), F32), jax.ShapeDtypeStruct((m, d), F32)),
        grid=(m // tm,),
        in_specs=[blk, blk, blk, pl.BlockSpec((tm, ROUTER_LANES), lambda i: (i, 0)),
                  pl.BlockSpec((1, d), lambda i: (0, 0))],
        out_specs=(blk, blk),
        compiler_params=_cparams(1),
        name="combine_norm",
    )(x, y0, y1, gates, g_final.reshape(1, d))


def moe_layer(x, g, w_router, b_router, wg, wu, wd, g_final, *, tm, bm, tf, y_dtype):
    m, d = x.shape
    wr = jnp.zeros((d, ROUTER_LANES), BF16).at[:, :N_EXPERTS].set(w_router.astype(BF16))
    br = jnp.full((1, ROUTER_LANES), NEG_INF, F32).at[0, :N_EXPERTS].set(b_router)
    idx, gates = moe_router(x, g, wr, br, tm=tm)

    nk = m * TOP_K
    flat_e = idx[:, :TOP_K].reshape(nk)
    onehot = (flat_e[:, None] == jnp.arange(N_EXPERTS, dtype=jnp.int32)[None, :]).astype(jnp.int32)
    csum = jnp.cumsum(onehot, axis=0)
    counts = csum[-1]
    rank = jnp.take_along_axis(csum, flat_e[:, None], axis=1)[:, 0] - 1
    padded = (counts + bm - 1) // bm * bm
    pad_end = jnp.cumsum(padded)
    pad_start = pad_end - padded
    dest = pad_start[flat_e] + rank
    l_rows = (nk + bm - 1) // bm * bm + N_EXPERTS * bm
    n_blocks = l_rows // bm
    tok_buf = jnp.zeros((l_rows,), jnp.int32).at[dest].set(jnp.arange(nk, dtype=jnp.int32) // TOP_K)
    block_expert = jnp.clip(
        jnp.searchsorted(pad_end, jnp.arange(n_blocks, dtype=jnp.int32) * bm, side="right"), 0, N_EXPERTS - 1
    ).astype(jnp.int32)
    n_used = (pad_end[-1] // bm).astype(jnp.int32).reshape(1)

    y_buf = moe_ffn_grouped(block_expert, n_used, tok_buf, x, g, wg, wu, wd, bm=bm, tf=tf, out_dtype=y_dtype)
    dest2 = dest.reshape(m, TOP_K)
    y0 = jnp.take(y_buf, dest2[:, 0], axis=0)
    y1 = jnp.take(y_buf, dest2[:, 1], axis=0)
    return combine_norm(x, y0, y1, gates, g_final, tm=tm)


def _sample_mix_weights(sgu_w, sgu_b, t_new, n_seq):
    r = jnp.arange(n_seq * t_new)
    t = r % t_new
    same = (r[:, None] // t_new) == (r[None, :] // t_new)
    causal = t[None, :] <= t[:, None]
    w = sgu_w[:, t[:, None], t[None, :]] * (same & causal)[None].astype(sgu_w.dtype)
    bias = jnp.repeat(sgu_b[:, t].T, CHUNK, axis=1)
    return w, bias


def kernel(x_prompt, x_sample, mem_prompt, cache_attn_k, cache_attn_v, cache_mem_k, cache_mem_v, page_table, norm_mix, w_in, b_gate, sgu_ln_g, sgu_ln_b, sgu_w, sgu_b, lam_q1, lam_k1, lam_q2, lam_k2, subln_g, mem_norm, w_mem_k, w_mem_v, w_br_a, w_br_b, w_br_m, w_out, norm_ffn, w_d_gate, w_d_up, w_d_down, w_router, b_router, w_e_gate, w_e_up, w_e_down, norm_final):
    batch, seq, d = x_prompt.shape
    n_seq, t_new, _ = x_sample.shape
    depth = w_in.shape[0]
    n_pages = page_table.shape[1]
    past_len = n_pages * PAGE_SIZE
    mp = batch * seq
    ms = n_seq * t_new
    assert ms == CHUNK and d == D_MODEL and N_QROWS == 2 * DA_HEADS * t_new

    xp = x_prompt.reshape(mp, d)
    xs = x_sample.reshape(ms, d)
    mem = mem_prompt.reshape(batch * N_MEM, d)
    slopes =jnp.asarray([2.0 ** (-8.0 * (i + 1) / DA_HEADS) for i in range(DA_HEADS)], F32)
    row_slopes = jnp.repeat(slopes, 2 * t_new).reshape(N_QROWS, 1)
    tril = jnp.tril(jnp.ones((CHUNK, CHUNK), F32))

    c_q = 2 * SGU_WIDTH
    c_k = c_q + DA_WIDTH
    c_qm = c_k + 2 * DA_WIDTH
    c_gate = c_qm + MEM_WIDTH

    outs = {k: [] for k in ("pk", "pv", "pmk", "pmv", "sk", "sv", "scv")}
    yp = ys = None
    for l in range(depth):
        lam_init = 0.8 - 0.6 * math.exp(-0.3 * l)
        lamp = jnp.stack([lam_q1[l], lam_k1[l], lam_q2[l], lam_k2[l]])
        w_l = w_in[l]
        w_z = jnp.concatenate([w_l[:, :c_k], w_l[:, c_qm:c_gate]], axis=1).astype(BF16)
        w_kv = w_l[:, c_k:c_qm].astype(BF16)
        w_g = w_l[:, c_gate:].astype(BF16)
        zero_bias = jnp.zeros((w_z.shape[1],), F32)
        w_mem = jnp.concatenate([w_mem_k[l], w_mem_v[l]], axis=1).astype(BF16)
        wa, wb, wm, wo = (w.astype(BF16) for w in (w_br_a[l], w_br_b[l], w_br_m[l], w_out[l]))
        w_mix_p = (sgu_w[l] * tril[None]).astype(BF16)
        bias_p = jnp.repeat(sgu_b[l].T, CHUNK, axis=1)
        w_mix_s, bias_s = _sample_mix_weights(sgu_w[l], sgu_b[l], t_new, n_seq)
        qm_blk = c_k // MEM_WIDTH

        mkv = norm_matmul(mem, mem_norm[l], w_mem, jnp.zeros((2 * MEM_WIDTH,), F32), sigmoid=False,
                          out_dtype=F32, tm=batch * N_MEM, tn=MEM_WIDTH)
        mk_p = mkv[:, :MEM_WIDTH].reshape(batch, N_MEM, MEM_WIDTH)
        mv_p = mkv[:, MEM_WIDTH:].reshape(batch, N_MEM, MEM_WIDTH)

        z = norm_matmul(xp, norm_mix[l], w_z, zero_bias, sigmoid=False, out_dtype=BF16, tm=1024, tn=1280)
        kf, vf, kb, vt = norm_kv(xp, norm_mix[l], w_kv, tm=512, batch=batch)
        gates = norm_matmul(xp, norm_mix[l], w_g, b_gate[l], sigmoid=True, out_dtype=BF16, tm=1024, tn=1024)
        (a,) = sgu(z, sgu_ln_g[l], sgu_ln_b[l], w_mix_p, bias_p, n_chunks=4, emit_v=False)
        bo = diff_attn_prompt(z, kb, vt, slopes, lamp, subln_g[l], batch=batch, seq=seq, lam_init=lam_init,
                              q_col0=c_q // DA_V_DIM, tq=1024)
        mo = mem_attn(z, mk_p, mv_p, n_batch=batch, rows_per_batch=seq, tq=512, q_col0=qm_blk)
        xp = merge(xp, a, bo, mo, gates, wa, wb, wm, wo, tm=512)

        zs = norm_matmul(xs, norm_mix[l], w_z, zero_bias, sigmoid=False, out_dtype=F32, tm=ms, tn=512)
        kfs, vfs, _, _ = norm_kv(xs, norm_mix[l], w_kv, tm=ms, batch=1)
        gates_s = norm_matmul(xs, norm_mix[l], w_g, b_gate[l], sigmoid=True, out_dtype=F32, tm=ms, tn=512)
        a_s, vrows = sgu(zs, sgu_ln_g[l], sgu_ln_b[l], w_mix_s.astype(BF16), bias_s, n_chunks=1, emit_v=True)
        q_s = zs[:, c_q:c_k].reshape(n_seq, 1, t_new, DA_WIDTH)
        q_rep = jnp.broadcast_to(q_s, (n_seq, 2 * DA_HEADS, t_new, DA_WIDTH)).reshape(n_seq, N_QROWS, DA_WIDTH)
        bo_s = diff_attn_sample(page_table, q_rep, kfs.reshape(n_seq, t_new, DA_WIDTH),
                                vfs.reshape(n_seq, t_new, DA_WIDTH), cache_attn_k, cache_attn_v, lamp, row_slopes,
                                subln_g[l], layer=l, past_len=past_len, lam_init=lam_init, pages_per_step=8)
        qm_s = zs[:, c_k:c_k + MEM_WIDTH].reshape(n_seq, t_new, MEM_WIDTH)
        qm_s = jnp.pad(qm_s, ((0, 0), (0, QM_PAD_ROWS - t_new), (0, 0))).reshape(n_seq * QM_PAD_ROWS, MEM_WIDTH)
        mo_s = mem_attn(qm_s, cache_mem_k[l].reshape(n_seq, N_MEM, MEM_WIDTH),
                        cache_mem_v[l].reshape(n_seq, N_MEM, MEM_WIDTH), n_batch=n_seq,
                        rows_per_batch=QM_PAD_ROWS, tq=QM_PAD_ROWS, q_col0=0)
        mo_s = mo_s.reshape(n_seq, QM_PAD_ROWS, MEM_WIDTH)[:, :t_new].reshape(ms, MEM_WIDTH)
        xs = merge(xs, a_s, bo_s.reshape(ms, DA_WIDTH), mo_s, gates_s, wa, wb, wm, wo, tm=ms)

        if l % 2 == 0:
            i = l // 2
            wg, wu, wd = w_d_gate[i].astype(BF16), w_d_up[i].astype(BF16), w_d_down[i].astype(BF16)
            xp = ffn_dense(xp, norm_ffn[l], wg, wu, wd, tm=512, tf=1408)
            xs = ffn_dense(xs, norm_ffn[l], wg, wu, wd, tm=ms, tf=1408)
        else:
            i = l // 2
            wg, wu, wd = w_e_gate[i].astype(BF16), w_e_up[i].astype(BF16), w_e_down[i].astype(BF16)
            xp, yp = moe_layer(xp, norm_ffn[l], w_router[i], b_router[i], wg, wu, wd, norm_final, tm=512, bm=512,
                               tf=1792, y_dtype=BF16)
            xs, ys = moe_layer(xs, norm_ffn[l], w_router[i], b_router[i], wg, wu, wd, norm_final, tm=ms, bm=128,
                               tf=1792, y_dtype=F32)

        outs["pk"].append(kf.reshape(batch, seq, DA_HEADS, DA_V_DIM))
        outs["pv"].append(vf.reshape(batch, seq, DA_HEADS, DA_V_DIM))
        outs["pmk"].append(mk_p.reshape(batch, N_MEM, MEM_HEADS, MEM_HEAD_DIM))
        outs["pmv"].append(mv_p.reshape(batch, N_MEM, MEM_HEADS, MEM_HEAD_DIM))
        outs["sk"].append(kfs.reshape(n_seq, t_new, DA_HEADS, DA_V_DIM))
        outs["sv"].append(vfs.reshape(n_seq, t_new, DA_HEADS, DA_V_DIM))
        outs["scv"].append(vrows.reshape(n_seq, t_new, SGU_WIDTH))

    if depth % 2 == 1:
        raise NotImplementedError("final norm is fused into the expert layer, which must come last")
    return (yp.reshape(batch, seq, d), ys.reshape(n_seq, t_new, d),
            jnp.stack(outs["pk"]), jnp.stack(outs["pv"]), jnp.stack(outs["pmk"]), jnp.stack(outs["pmv"]),
            jnp.stack(outs["sk"]), jnp.stack(outs["sv"]), jnp.stack(outs["scv"]))
```

```python
import functools
import math

import jax
import jax.numpy as jnp
from jax import lax
from jax.experimental import pallas as pl
from jax.experimental.pallas import tpu as pltpu

F32 = jnp.float32
BF16 = jnp.bfloat16

D_MODEL = 1024
CHUNK = 128
SGU_GROUPS = 4
SGU_WIDTH = 512
DA_HEADS = 8
DA_QK_DIM = 64
DA_V_DIM = 128
DA_WIDTH = DA_HEADS * DA_V_DIM
MEM_HEADS = 4
MEM_HEAD_DIM = 128
MEM_WIDTH = MEM_HEADS * MEM_HEAD_DIM
N_MEM = 256
N_EXPERTS = 8
TOP_K = 2
PAGE_SIZE = 128
EPS = 1e-5
NEG_INF = float("-inf")

VMEM_LIMIT_BYTES = 48 * 1024 * 1024
LANES = 128
ROUTER_LANES = 128
QM_PAD_ROWS = 16
ALIBI_SPLIT = 16
SUM_ROWS = 16
LOG2E = 1.4426950408889634
ROW_DMA_UNROLL = 8


def _cparams(n_axes):
    return pltpu.CompilerParams(
        dimension_semantics=("arbitrary",) * n_axes, vmem_limit_bytes=VMEM_LIMIT_BYTES)


def _rms(x, g):
    ms = jnp.mean(x * x, axis=-1, keepdims=True)
    return x * lax.rsqrt(ms + EPS) * g


def _gelu(x):
    return 0.5 * x * (1.0 + lax.erf(x * (1.0 / math.sqrt(2.0))))


def _sigmoid(x):
    return 0.5 * (jnp.tanh(0.5 * x) + 1.0)


def _dot(a, b):
    return jnp.dot(a, b, preferred_element_type=F32)


def _dot_nt(a, b):
    return lax.dot_general(a, b, (((1,), (1,)), ((), ())), preferred_element_type=F32)


def _norm_matmul_kernel(x_ref, g_ref, w_ref, b_ref, o_ref, h_scr, *, sigmoid):
    @pl.when(pl.program_id(1) == 0)
    def _():
        h_scr[...] = _rms(x_ref[...], g_ref[...]).astype(BF16)

    acc = _dot(h_scr[...], w_ref[...])
    if sigmoid:
        acc = _sigmoid(acc + b_ref[...])
    o_ref[...] = acc.astype(o_ref.dtype)


def norm_matmul(x, g, w, bias, *, sigmoid, out_dtype, tm, tn):
    m, d = x.shape
    n = w.shape[1]
    return pl.pallas_call(
        functools.partial(_norm_matmul_kernel, sigmoid=sigmoid),
        out_shape=jax.ShapeDtypeStruct((m, n), out_dtype),
        grid=(m // tm, n // tn),
        in_specs=[
            pl.BlockSpec((tm, d), lambda i, j: (i, 0)),
            pl.BlockSpec((1, d), lambda i, j: (0, 0)),
            pl.BlockSpec((d, tn), lambda i, j: (0, j)),
            pl.BlockSpec((1, tn), lambda i, j: (0, j)),
        ],
        out_specs=pl.BlockSpec((tm, tn), lambda i, j: (i, j)),
        scratch_shapes=[pltpu.VMEM((tm, d), BF16)],
        compiler_params=_cparams(2),
        name="norm_matmul_sig" if sigmoid else "norm_matmul",
    )(x, g.reshape(1, d), w, bias.reshape(1, n))


def _norm_kv_kernel(x_ref, g_ref, w_ref, kf_ref, vf_ref, kb_ref, vt_ref, h_scr):
    j = pl.program_id(1)

    @pl.when(j == 0)
    def _():
        h_scr[...] = _rms(x_ref[...], g_ref[...]).astype(BF16)

    acc = _dot(h_scr[...], w_ref[...])

    @pl.when(j == 0)
    def _():
        kf_ref[...] = acc
        kb_ref[...] = acc.astype(BF16)

    @pl.when(j == 1)
    def _():
        vf_ref[...] = acc
        vt_ref[...] = acc.T.astype(BF16)


def norm_kv(x, g, w_kv, *, tm, batch):
    m, d = x.shape
    n = w_kv.shape[1] // 2
    nb = m // batch // tm
    blk = pl.BlockSpec((tm, n), lambda i, j: (i, 0))
    return pl.pallas_call(
        _norm_kv_kernel,
        out_shape=(jax.ShapeDtypeStruct((m, n), F32), jax.ShapeDtypeStruct((m, n), F32),
                   jax.ShapeDtypeStruct((m, n), BF16), jax.ShapeDtypeStruct((batch, n, m // batch), BF16)),
        grid=(m // tm, 2),
        in_specs=[
            pl.BlockSpec((tm, d), lambda i, j: (i, 0)),
            pl.BlockSpec((1, d), lambda i, j: (0, 0)),
            pl.BlockSpec((d, n), lambda i, j: (0, j)),
        ],
        out_specs=(blk, blk, blk, pl.BlockSpec((None, n, tm), lambda i, j: (i // nb, 0, i % nb))),
        scratch_shapes=[pltpu.VMEM((tm, d), BF16)],
        compiler_params=_cparams(2),
        name="norm_kv",
    )(x, g.reshape(1, d), w_kv)


def _sgu_kernel(u_ref, v_ref, lng_ref, lnb_ref, w_ref, bias_ref, a_ref, *maybe_vn_ref, n_chunks):
    u = _gelu(u_ref[...].astype(F32))
    v = _gelu(v_ref[...].astype(F32))
    mu = jnp.mean(v, axis=-1, keepdims=True)
    vc = v - mu
    var = jnp.mean(vc * vc, axis=-1, keepdims=True)
    vn = vc * lax.rsqrt(var + EPS) * lng_ref[...] + lnb_ref[...]
    if maybe_vn_ref:
        maybe_vn_ref[0][...] = vn
    vnb = vn.astype(BF16)
    bias = bias_ref[...]
    for c in range(n_chunks):
        rows = slice(c * CHUNK, (c + 1) * CHUNK)
        for g in range(SGU_GROUPS):
            cols = slice(g * CHUNK, (g + 1) * CHUNK)
            mixed = _dot(w_ref[g], vnb[rows, cols]) + bias[:, cols]
            a_ref[rows, cols] = (u[rows, cols] * mixed).astype(a_ref.dtype)


def sgu(z, ln_g, ln_b, w_mix, bias_full, *, n_chunks, emit_v):
    m = z.shape[0]
    rows = n_chunks * CHUNK
    row_blk = pl.BlockSpec((rows, SGU_WIDTH), lambda i: (i, 0))
    out_shape = [jax.ShapeDtypeStruct((m, SGU_WIDTH), BF16)]
    out_specs = [row_blk]
    if emit_v:
        out_shape.append(jax.ShapeDtypeStruct((m, SGU_WIDTH), F32))
        out_specs.append(row_blk)
    return pl.pallas_call(
        functools.partial(_sgu_kernel, n_chunks=n_chunks),
        out_shape=tuple(out_shape),
        grid=(m // rows,),
        in_specs=[
            row_blk,
            pl.BlockSpec((rows, SGU_WIDTH), lambda i: (i, 1)),
            pl.BlockSpec((1, SGU_WIDTH), lambda i: (0, 0)),
            pl.BlockSpec((1, SGU_WIDTH), lambda i: (0, 0)),
            pl.BlockSpec((SGU_GROUPS, CHUNK, CHUNK), lambda i: (0, 0, 0)),
            pl.BlockSpec((CHUNK, SGU_WIDTH), lambda i: (0, 0)),
        ],
        out_specs=tuple(out_specs),
        compiler_params=_cparams(1),
        name="sgu",
    )(z, z, ln_g.reshape(1, -1), ln_b.reshape(1, -1), w_mix, bias_full)


def _diff_lambda(lamp, lam_init):
    s1 = jnp.sum(lamp[0:1, :] * lamp[1:2, :], axis=-1, keepdims=True)
    s2 = jnp.sum(lamp[2:3, :] * lamp[3:4, :], axis=-1, keepdims=True)
    return jnp.exp(s1) - jnp.exp(s2) + lam_init


def _sub_norm(o, g, lam_init):
    return _rms(o, g) * (1.0 - lam_init)


def _diff_attn_kernel(slopes_ref, lamp_ref, q_ref, k_ref, vt_ref, g_ref, o_ref, causal_scr, q_alibi_scr,
                      k_alibi_scr, *, tq, lam_init):
    h = pl.program_id(1)
    qi = pl.program_id(2)
    slope2 = slopes_ref[h] * LOG2E

    @pl.when((pl.program_id(0) == 0) & (h == 0) & (qi == 0))
    def _():
        krow = lax.broadcasted_iota(jnp.int32, (tq, tq), 0)
        qcol = lax.broadcasted_iota(jnp.int32, (tq, tq), 1)
        causal_scr[...] = jnp.where(krow <= qcol, 0.0, NEG_INF)

    q = (q_ref[...].astype(F32) * (DA_QK_DIM ** -0.5 * LOG2E)).astype(BF16)
    lane = lax.broadcasted_iota(jnp.int32, (tq, LANES), 1)

    @pl.when(qi == 0)
    def _():
        pos = lax.broadcasted_iota(jnp.int32, (tq, LANES), 0)
        hi = (pos // ALIBI_SPLIT).astype(F32)
        lo = (pos % ALIBI_SPLIT).astype(F32)
        c_hi = slope2.astype(BF16).astype(F32)
        c_lo = (slope2 - c_hi).astype(BF16).astype(F32)
        qa = jnp.zeros((tq, LANES), F32)
        ka = jnp.zeros((tq, LANES), F32)
        for n, c in enumerate((c_hi, c_lo)):
            qa = jnp.where(lane == 4 * n, c * ALIBI_SPLIT, jnp.where(lane == 4 * n + 1, c, jnp.where(
                lane == 4 * n + 2, -c * ALIBI_SPLIT * hi, jnp.where(lane == 4 * n + 3, -c * lo, qa))))
            ka = jnp.where(lane == 4 * n, hi, jnp.where(lane == 4 * n + 1, lo, jnp.where(
                (lane == 4 * n + 2) | (lane == 4 * n + 3), 1.0, ka)))
        q_alibi_scr[...] = qa.astype(BF16)
        k_alibi_scr[...] = ka.astype(BF16)

    q_alibi = q_alibi_scr[...]
    k_alibi = k_alibi_scr[...]
    qzero = jnp.zeros_like(q)
    qz = (jnp.concatenate([jnp.where(lane < DA_QK_DIM, q, qzero), q_alibi], axis=1),
          jnp.concatenate([jnp.where(lane >= DA_QK_DIM, q, qzero), q_alibi], axis=1))
    ones = jnp.ones((SUM_ROWS, tq), BF16)

    def update(carry, start, shift, k0=0, nk=tq, c0=0, masked=False):
        rows = pl.ds(pl.multiple_of(start + k0, nk), nk)
        kt = jnp.concatenate([k_ref[rows, :], k_alibi[k0:k0 + nk]], axis=1)
        vt = jnp.concatenate([vt_ref[:, rows], ones[:, :nk]], axis=0)
        ts = [_dot_nt(kt, qm[c0:]) for qm in qz]
        new = []
        for (m, acc), t in zip(carry, ts):
            if masked:
                t = t + causal_scr[k0:k0 + nk, c0:]
            m_new = jnp.maximum(m[:, c0:], jnp.max(t, axis=0, keepdims=True) + shift)
            p = jnp.exp2(t - (m_new - shift))
            acc_new = jnp.exp2(m[:, c0:] - m_new) * acc[:, c0:] + _dot(vt, p.astype(BF16))
            if c0:
                m_new = jnp.concatenate([m[:, :c0], m_new], axis=1)
                acc_new = jnp.concatenate([acc[:, :c0], acc_new], axis=1)
            new.append((m_new, acc_new))
        return tuple(new)

    def body(j, carry):
        shift = ((j - qi) * tq).astype(F32) * slope2
        return update(carry, j * tq, shift)

    one = (jnp.full((1, tq), NEG_INF, F32), jnp.zeros((DA_V_DIM + SUM_ROWS, tq), F32))
    carry = lax.fori_loop(0, qi, body, (one, one))
    half = tq // 2
    carry = update(carry, qi * tq, 0.0, k0=0, nk=half, c0=0, masked=True)
    (_, a1), (_, a2) = update(carry, qi * tq, 0.0, k0=half, nk=half, c0=half, masked=True)
    lam = _diff_lambda(lamp_ref[...], lam_init)
    o1 = a1[:DA_V_DIM] / a1[DA_V_DIM:DA_V_DIM + 1]
    o2 = a2[:DA_V_DIM] / a2[DA_V_DIM:DA_V_DIM + 1]
    o = (o1 - lam * o2).T
    o_ref[...] = _sub_norm(o, g_ref[...], lam_init).astype(o_ref.dtype)


def diff_attn_prompt(z, kb, vt, slopes, lamp, subln_g, *, batch, seq, lam_init, q_col0, tq):
    nq = seq // tq
    grid_spec = pltpu.PrefetchScalarGridSpec(
        num_scalar_prefetch=1,
        grid=(batch, DA_HEADS, nq),
        in_specs=[
            pl.BlockSpec((4, DA_QK_DIM), lambda b, h, i, s: (0, 0)),
            pl.BlockSpec((tq, DA_V_DIM), lambda b, h, i, s: (b * nq + i, q_col0 + h)),
            pl.BlockSpec((seq, DA_V_DIM), lambda b, h, i, s: (b, h)),
            pl.BlockSpec((None, DA_V_DIM, seq), lambda b, h, i, s: (b, h, 0)),
            pl.BlockSpec((1, DA_V_DIM), lambda b, h, i, s: (0, 0)),
        ],
        out_specs=pl.BlockSpec((tq, DA_V_DIM), lambda b, h, i, s: (b * nq + i, h)),
        scratch_shapes=[pltpu.VMEM((tq, tq), F32),
                        pltpu.VMEM((tq, LANES), BF16), pltpu.VMEM((tq, LANES), BF16)],
    )
    return pl.pallas_call(
        functools.partial(_diff_attn_kernel, tq=tq, lam_init=lam_init),
        out_shape=jax.ShapeDtypeStruct((batch * seq, DA_WIDTH), BF16),
        grid_spec=grid_spec,
        compiler_params=_cparams(3),
        name="diff_attn_prompt",
    )(slopes, lamp, z, kb, vt, subln_g.reshape(1, -1))


N_QROWS = DA_HEADS * 2 * 4


def _paged_attn_kernel(pt_ref, lamp_ref, slope_ref, q_ref, kn_ref, vn_ref, g_ref, ck_hbm, cv_hbm, o_ref, kv_buf,
                       qz_scr, s_scr, sn_scr, m_scr, l_scr, acc_scr, sem, *, pages_per_step, t_new, past_len,
                       lam_init, layer):
    pp = pages_per_step
    b = pl.program_id(0)
    phase = pl.program_id(1)
    g = pl.program_id(2)
    n_seq = pl.num_programs(0)
    n_groups = pl.num_programs(2)
    last = n_groups - 1
    step_keys = pp * PAGE_SIZE
    rows_per_head = 2 * t_new
    slot = (phase * n_groups + g) % 2

    def tile_copy(cache_hbm, seq, grp, r, h, dst_slot):
        page_id = pt_ref[seq, grp * pp + r]
        return pltpu.make_async_copy(cache_hbm.at[layer, page_id, :, h, :], kv_buf.at[dst_slot, r, h],
                                     sem.at[dst_slot])

    def start_step(seq, ph, grp, dst_slot):
        for cache_hbm, which in ((ck_hbm, 0), (cv_hbm, 1)):
            @pl.when(ph == which)
            def _():
                for r in range(pp):
                    for h in range(DA_HEADS):
                        tile_copy(cache_hbm, seq, grp, r, h, dst_slot).start()

    def wait_step():
        for cache_hbm, which in ((ck_hbm, 0), (cv_hbm, 1)):
            @pl.when(phase == which)
            def _():
                for r in range(pp):
                    for h in range(DA_HEADS):
                        tile_copy(cache_hbm, b, g, r, h, slot).wait()

    @pl.when((b == 0) & (phase == 0) & (g == 0))
    def _():
        start_step(b, phase, g, slot)

    wrap = g == last
    nxt_g = jnp.where(wrap, 0, g + 1)
    nxt_phase = jnp.where(wrap, 1 - phase, phase)
    nxt_b = jnp.where(wrap & (phase == 1), b + 1, b)

    @pl.when(nxt_b < n_seq)
    def _():
        start_step(nxt_b, nxt_phase, nxt_g, 1 - slot)

    wait_step()

    rowi = lax.broadcasted_iota(jnp.int32, (N_QROWS, 1), 0)
    slope = slope_ref[...]
    tok = rowi % t_new
    lam = _diff_lambda(lamp_ref[...], lam_init)

    def page(r):
        return jnp.concatenate([kv_buf[slot, r, h].astype(BF16) for h in range(DA_HEADS)], axis=1)

    def combine(p):
        return pltpu.roll(p, t_new, axis=0) - lam * p

    def new_key_scores():
        kn = jnp.concatenate([kn_ref[...].astype(BF16), jnp.zeros((PAGE_SIZE - t_new, DA_WIDTH), BF16)], axis=0)
        s_all = _dot_nt(qz_scr[...], kn)
        row = lax.broadcasted_iota(jnp.int32, s_all.shape, 0)
        j = lax.broadcasted_iota(jnp.int32, s_all.shape, 1)
        valid = (j < t_new) & (j <= row % t_new)
        return jnp.where(valid, s_all - slope * (row % t_new - j).astype(F32), NEG_INF)

    @pl.when(jnp.logical_and(phase == 0, g == 0))
    def _():
        row = lax.broadcasted_iota(jnp.int32, (N_QROWS, DA_WIDTH), 0)
        lane = lax.broadcasted_iota(jnp.int32, (N_QROWS, DA_WIDTH), 1)
        keep = (lane // DA_QK_DIM) == (row // t_new)
        qz_scr[...] = jnp.where(keep, q_ref[...] * (DA_QK_DIM ** -0.5), 0.0).astype(BF16)
        m_scr[...] = jnp.full(m_scr.shape, NEG_INF, F32)
        l_scr[...] = jnp.zeros(l_scr.shape, F32)
        acc_scr[...] = jnp.zeros(acc_scr.shape, F32)

    def online(s):
        m_old = m_scr[...]
        m_new = jnp.maximum(m_old, jnp.max(s, axis=-1, keepdims=True))
        l_scr[...] = jnp.exp(m_old - m_new) * l_scr[...] + jnp.sum(jnp.exp(s - m_new), axis=-1, keepdims=True)
        m_scr[...] = m_new

    @pl.when(phase == 0)
    def _():
        qz = qz_scr[...]
        s = jnp.concatenate([_dot_nt(qz, page(r)) for r in range(pp)], axis=1)
        kpos = g * step_keys + lax.broadcasted_iota(jnp.int32, (1, step_keys), 1)
        s = s - slope * ((past_len + tok) - kpos).astype(F32)
        s_scr[:, pl.ds(pl.multiple_of(g * step_keys, step_keys), step_keys)] = s
        online(s)

        @pl.when(g == last)
        def _():
            sn = new_key_scores()
            sn_scr[...] = sn
            online(sn)

    @pl.when(phase == 1)
    def _():
        s = s_scr[:, pl.ds(pl.multiple_of(g * step_keys, step_keys), step_keys)]
        m = m_scr[...]
        inv_l = 1.0 / l_scr[...]
        pc = combine(jnp.exp(s - m) * inv_l).astype(BF16)
        pv = _dot(pc[:, 0:PAGE_SIZE], page(0))
        for r in range(1, pp):
            pv += _dot(pc[:, r * PAGE_SIZE:(r + 1) * PAGE_SIZE], page(r))
        acc_scr[...] += pv

        @pl.when(g == last)
        def _():
            pn = combine(jnp.exp(sn_scr[...] - m) * inv_l).astype(BF16)
            vn = jnp.concatenate([vn_ref[...].astype(BF16), jnp.zeros((PAGE_SIZE - t_new, DA_WIDTH), BF16)], axis=0)
            acc = acc_scr[...] + _dot(pn, vn)
            outs = []
            for h in range(DA_HEADS):
                r0 = h * rows_per_head + t_new
                outs.append(_sub_norm(acc[r0:r0 + t_new, h * DA_V_DIM:(h + 1) * DA_V_DIM], g_ref[...], lam_init))
            o_ref[...] = jnp.concatenate(outs, axis=1)


def diff_attn_sample(page_table, q_rep, k_new, v_new, cache_k, cache_v, lamp, row_slopes, subln_g, *, layer,
                     past_len, lam_init, pages_per_step):
    n_seq, t_new = k_new.shape[:2]
    n_pages = page_table.shape[1]
    pp = pages_per_step
    n_groups = n_pages // pp
    assert n_pages % pp == 0
    new_spec = pl.BlockSpec((None, t_new, DA_WIDTH), lambda b, ph, g, pt: (b, 0, 0))
    grid_spec = pltpu.PrefetchScalarGridSpec(
        num_scalar_prefetch=1,
        grid=(n_seq, 2, n_groups),
        in_specs=[
            pl.BlockSpec((4, DA_QK_DIM), lambda b, ph, g, pt: (0, 0)),
            pl.BlockSpec((N_QROWS, 1), lambda b, ph, g, pt: (0, 0)),
            pl.BlockSpec((None, N_QROWS, DA_WIDTH), lambda b, ph, g, pt: (b, 0, 0)),
            new_spec, new_spec,
            pl.BlockSpec((1, DA_V_DIM), lambda b, ph, g, pt: (0, 0)),
            pl.BlockSpec(memory_space=pl.ANY),
            pl.BlockSpec(memory_space=pl.ANY),
        ],
        out_specs=pl.BlockSpec((None, t_new, DA_WIDTH), lambda b, ph, g, pt: (b, 0, 0)),
        scratch_shapes=[
            pltpu.VMEM((2, pp, DA_HEADS, PAGE_SIZE, DA_V_DIM), F32),
            pltpu.VMEM((N_QROWS, DA_WIDTH), BF16),
            pltpu.VMEM((N_QROWS, n_pages * PAGE_SIZE), F32),
            pltpu.VMEM((N_QROWS, PAGE_SIZE), F32),
            pltpu.VMEM((N_QROWS, 1), F32),
            pltpu.VMEM((N_QROWS, 1), F32),
            pltpu.VMEM((N_QROWS, DA_WIDTH), F32),
            pltpu.SemaphoreType.DMA((2,)),
        ],
    )
    return pl.pallas_call(
        functools.partial(_paged_attn_kernel, pages_per_step=pp, t_new=t_new, past_len=past_len, lam_init=lam_init,
                          layer=layer),
        out_shape=jax.ShapeDtypeStruct((n_seq, t_new, DA_WIDTH), F32),
        grid_spec=grid_spec,
        compiler_params=_cparams(3),
        name="diff_attn_sample",
    )(page_table, lamp, row_slopes, q_rep, k_new, v_new, subln_g.reshape(1, -1), cache_k, cache_v)


def _mem_attn_kernel(q_ref, mk_ref, mv_ref, o_ref):
    scale = MEM_HEAD_DIM ** -0.5
    q = q_ref[...].astype(BF16)
    mk = mk_ref[...].astype(BF16)
    mv = mv_ref[...].astype(BF16)
    outs = []
    for h in range(MEM_HEADS):
        cols = slice(h * MEM_HEAD_DIM, (h + 1) * MEM_HEAD_DIM)
        s = _dot_nt(q[:, cols], mk[:, cols]) * scale
        e = jnp.exp(s - jnp.max(s, axis=-1, keepdims=True))
        p = e / jnp.sum(e, axis=-1, keepdims=True)
        outs.append(_dot(p.astype(BF16), mv[:, cols]))
    o_ref[...] = jnp.concatenate(outs, axis=1).astype(o_ref.dtype)


def mem_attn(q_arr, mk, mv, *, n_batch, rows_per_batch, tq, q_col0):
    nq = rows_per_batch // tq
    return pl.pallas_call(
        _mem_attn_kernel,
        out_shape=jax.ShapeDtypeStruct((n_batch * rows_per_batch, MEM_WIDTH), BF16),
        grid=(n_batch, nq),
        in_specs=[
            pl.BlockSpec((tq, MEM_WIDTH), lambda b, i: (b * nq + i, q_col0)),
            pl.BlockSpec((None, N_MEM, MEM_WIDTH), lambda b, i: (b, 0, 0)),
            pl.BlockSpec((None, N_MEM, MEM_WIDTH), lambda b, i: (b, 0, 0)),
        ],
        out_specs=pl.BlockSpec((tq, MEM_WIDTH), lambda b, i: (b * nq + i, 0)),
        compiler_params=_cparams(2),
        name="mem_attn",
    )(q_arr, mk, mv)


def _merge_kernel(x_ref, a_ref, b_ref, m_ref, gate_ref, wa_ref, wb_ref, wm_ref, wo_ref, o_ref):
    d = D_MODEL
    ga = gate_ref[:, 0:d]
    gb = gate_ref[:, d:2 * d]
    gm = gate_ref[:, 2 * d:3 * d]
    y = ga.astype(F32) * _dot(a_ref[...].astype(BF16), wa_ref[...])
    y += gb.astype(F32) * _dot(b_ref[...].astype(BF16), wb_ref[...])
    y += gm.astype(F32) * _dot(m_ref[...].astype(BF16), wm_ref[...])
    o_ref[...] = x_ref[...] + _dot(y.astype(BF16), wo_ref[...])


def merge(x, a, b, mo, gates, wa, wb, wm, wo, *, tm):
    m, d = x.shape
    full = lambda arr: pl.BlockSpec(arr.shape, lambda i: (0, 0))
    rowblk = lambda arr: pl.BlockSpec((tm, arr.shape[1]), lambda i: (i, 0))
    return pl.pallas_call(
        _merge_kernel,
        out_shape=jax.ShapeDtypeStruct((m, d), F32),
        grid=(m // tm,),
        in_specs=[rowblk(x), rowblk(a), rowblk(b), rowblk(mo), rowblk(gates), full(wa), full(wb), full(wm), full(wo)],
        out_specs=pl.BlockSpec((tm, d), lambda i: (i, 0)),
        compiler_params=_cparams(1),
        name="merge",
    )(x, a, b, mo, gates, wa, wb, wm, wo)


def _silu(x):
    return x * _sigmoid(x)


def _ffn_kernel(x_ref, g_ref, wg_ref, wu_ref, wd_ref, o_ref, h_scr, acc_scr):
    f = pl.program_id(1)

    @pl.when(f == 0)
    def _():
        h_scr[...] = _rms(x_ref[...], g_ref[...]).astype(BF16)
        acc_scr[...] = jnp.zeros(acc_scr.shape, F32)

    h = h_scr[...]
    act = _silu(_dot(h, wg_ref[...])) * _dot(h, wu_ref[...])
    acc_scr[...] += _dot(act.astype(BF16), wd_ref[...])

    @pl.when(f == pl.num_programs(1) - 1)
    def _():
        o_ref[...] = x_ref[...] + acc_scr[...]


def ffn_dense(x, g, wg, wu, wd, *, tm, tf):
    m, d = x.shape
    f = wg.shape[1]
    return pl.pallas_call(
        _ffn_kernel,
        out_shape=jax.ShapeDtypeStruct((m, d), F32),
        grid=(m // tm, f // tf),
        in_specs=[
            pl.BlockSpec((tm, d), lambda i, j: (i, 0)),
            pl.BlockSpec((1, d), lambda i, j: (0, 0)),
            pl.BlockSpec((d, tf), lambda i, j: (0, j)),
            pl.BlockSpec((d, tf), lambda i, j: (0, j)),
            pl.BlockSpec((tf, d), lambda i, j: (j, 0)),
        ],
        out_specs=pl.BlockSpec((tm, d), lambda i, j: (i, 0)),
        scratch_shapes=[pltpu.VMEM((tm, d), BF16), pltpu.VMEM((tm, d), F32)],
        compiler_params=_cparams(2),
        name="ffn_dense",
    )(x, g.reshape(1, d), wg, wu, wd)


def _router_kernel(x_ref, g_ref, wr_ref, br_ref, idx_ref, gate_ref):
    h = _rms(x_ref[...], g_ref[...]).astype(BF16)
    logits = _dot(h, wr_ref[...]) + br_ref[...]
    lane_i = lax.broadcasted_iota(jnp.int32, logits.shape, 1)
    lane = lane_i.astype(F32)
    big = float(ROUTER_LANES)
    t1 = jnp.max(logits, axis=-1, keepdims=True)
    i1 = jnp.min(jnp.where(logits == t1, lane, big), axis=-1, keepdims=True)
    rest = jnp.where(lane == i1, NEG_INF, logits)
    t2 = jnp.max(rest, axis=-1, keepdims=True)
    i2 = jnp.min(jnp.where(rest == t2, lane, big), axis=-1, keepdims=True)
    e2 = jnp.exp(t2 - t1)
    g1 = 1.0 / (1.0 + e2)
    idx_ref[...] = jnp.where(lane_i == 0, i1, i2).astype(jnp.int32)
    gate_ref[...] = jnp.where(lane_i == 0, g1, e2 * g1)


def moe_router(x, g, w_router_pad, b_router_pad, *, tm):
    m, d = x.shape
    wide = pl.BlockSpec((tm, ROUTER_LANES), lambda i: (i, 0))
    return pl.pallas_call(
        _router_kernel,
        out_shape=(jax.ShapeDtypeStruct((m, ROUTER_LANES), jnp.int32),
                   jax.ShapeDtypeStruct((m, ROUTER_LANES), F32)),
        grid=(m // tm,),
        in_specs=[
            pl.BlockSpec((tm, d), lambda i: (i, 0)),
            pl.BlockSpec((1, d), lambda i: (0, 0)),
            pl.BlockSpec((d, ROUTER_LANES), lambda i: (0, 0)),
            pl.BlockSpec((1, ROUTER_LANES), lambda i: (0, 0)),
        ],
        out_specs=(wide, wide),
        compiler_params=_cparams(1),
        name="moe_router",
    )(x, g.reshape(1, d), w_router_pad, b_router_pad)


def _moe_ffn_kernel(be_ref, nused_ref, tok_ref, tok_next_ref, x_hbm, g_ref, wg_ref, wu_ref, wd_ref, o_ref, xbuf,
                    h_scr, acc_scr, sem):
    b = pl.program_id(0)
    f = pl.program_id(1)
    n_used = nused_ref[0]
    bm = xbuf.shape[1]

    def row_copy(tok, r, slot):
        return pltpu.make_async_copy(x_hbm.at[pl.ds(tok[0, r], 1), :], xbuf.at[slot, pl.ds(r, 1), :], sem.at[slot])

    def start_rows(tok, slot):
        def body(i, carry):
            for u in range(ROW_DMA_UNROLL):
                row_copy(tok, i * ROW_DMA_UNROLL + u, slot).start()
            return carry
        lax.fori_loop(0, bm // ROW_DMA_UNROLL, body, 0)

    def wait_rows(tok, slot):
        pltpu.make_async_copy(x_hbm.at[pl.ds(0, bm), :], xbuf.at[slot], sem.at[slot]).wait()

    @pl.when(b < n_used)
    def _():
        @pl.when(f == 0)
        def _():
            slot = b % 2

            @pl.when(b == 0)
            def _():
                start_rows(tok_ref, 0)

            wait_rows(tok_ref, slot)

            @pl.when(b + 1 < n_used)
            def _():
                start_rows(tok_next_ref, 1 - slot)

            h_scr[...] = _rms(xbuf[slot], g_ref[...]).astype(BF16)
            acc_scr[...] = jnp.zeros(acc_scr.shape, F32)

        h = h_scr[...]
        act = _silu(_dot(h, wg_ref[...])) * _dot(h, wu_ref[...])
        acc_scr[...] += _dot(act.astype(BF16), wd_ref[...])

        @pl.when(f == pl.num_programs(1) - 1)
        def _():
            o_ref[...] = acc_scr[...].astype(o_ref.dtype)

    @pl.when(jnp.logical_and(b >= n_used, f == pl.num_programs(1) - 1))
    def _():
        o_ref[...] = jnp.zeros(o_ref.shape, o_ref.dtype)


def moe_ffn_grouped(block_expert, n_used, tok_buf, x, g, wg, wu, wd, *, bm, tf, out_dtype):
    l = tok_buf.shape[0]
    d = x.shape[1]
    n_blocks = l // bm
    n_f = wg.shape[2] // tf
    tok3 = tok_buf.reshape(n_blocks, 1, bm)

    def fidx(b, f, nused):
        return jnp.where(b < nused[0], f, n_f - 1)

    grid_spec = pltpu.PrefetchScalarGridSpec(
        num_scalar_prefetch=2,
        grid=(n_blocks, n_f),
        in_specs=[
            pl.BlockSpec((None, 1, bm), lambda b, f, be, nu: (b, 0, 0), memory_space=pltpu.SMEM),
            pl.BlockSpec((None, 1, bm), lambda b, f, be, nu: (jnp.minimum(b + 1, n_blocks - 1), 0, 0),
                         memory_space=pltpu.SMEM),
            pl.BlockSpec(memory_space=pl.ANY),
            pl.BlockSpec((1, d), lambda b, f, be, nu: (0, 0)),
            pl.BlockSpec((None, d, tf), lambda b, f, be, nu: (be[b], 0, fidx(b, f, nu))),
            pl.BlockSpec((None, d, tf), lambda b, f, be, nu: (be[b], 0, fidx(b, f, nu))),
            pl.BlockSpec((None, tf, d), lambda b, f, be, nu: (be[b], fidx(b, f, nu), 0)),
        ],
        out_specs=pl.BlockSpec((bm, d), lambda b, f, be, nu: (b, 0)),
        scratch_shapes=[
            pltpu.VMEM((2, bm, d), F32),
            pltpu.VMEM((bm, d), BF16),
            pltpu.VMEM((bm, d), F32),
            pltpu.SemaphoreType.DMA((2,)),
        ],
    )
    return pl.pallas_call(
        _moe_ffn_kernel,
        out_shape=jax.ShapeDtypeStruct((l, d), out_dtype),
        grid_spec=grid_spec,
        compiler_params=_cparams(2),
        name="moe_ffn_grouped",
    )(block_expert, n_used, tok3, tok3, x, g.reshape(1, d), wg, wu, wd)


def _combine_norm_kernel(x_ref, y0_ref, y1_ref, gate_ref, g_ref, xo_ref, yo_ref):
    gate = gate_ref[...]
    x = x_ref[...] + gate[:, 0:1] * y0_ref[...].astype(F32) + gate[:, 1:2] * y1_ref[...].astype(F32)
    xo_ref[...] = x
    yo_ref[...] = _rms(x, g_ref[...])


def combine_norm(x, y0, y1, gates, g_final, *, tm):
    m, d = x.shape
    blk = pl.BlockSpec((tm, d), lambda i: (i, 0))
    return pl.pallas_call(
        _combine_norm_kernel,
        out_shape=(jax.ShapeDtypeStruct((m, d), F32), jax.ShapeDtypeStruct((m, d), F32)),
        grid=(m // tm,),
        in_specs=[blk, blk, blk, pl.BlockSpec((tm, ROUTER_LANES), lambda i: (i, 0)),
                  pl.BlockSpec((1, d), lambda i: (0, 0))],
        out_specs=(blk, blk),
        compiler_params=_cparams(1),
        name="combine_norm",
    )(x, y0, y1, gates, g_final.reshape(1, d))


def moe_layer(x, g, w_router, b_router, wg, wu, wd, g_final, *, tm, bm, tf, y_dtype):
    m, d = x.shape
    wr = jnp.zeros((d, ROUTER_LANES), BF16).at[:, :N_EXPERTS].set(w_router.astype(BF16))
    br = jnp.full((1, ROUTER_LANES), NEG_INF, F32).at[0, :N_EXPERTS].set(b_router)
    idx, gates = moe_router(x, g, wr, br, tm=tm)

    nk = m * TOP_K
    flat_e = idx[:, :TOP_K].reshape(nk)
    onehot = (flat_e[:, None] == jnp.arange(N_EXPERTS, dtype=jnp.int32)[None, :]).astype(jnp.int32)
    csum = jnp.cumsum(onehot, axis=0)
    counts = csum[-1]
    rank = jnp.take_along_axis(csum, flat_e[:, None], axis=1)[:, 0] - 1
    padded = (counts + bm - 1) // bm * bm
    pad_end = jnp.cumsum(padded)
    pad_start = pad_end - padded
    dest = pad_start[flat_e] + rank
    l_rows = (nk + bm - 1) // bm * bm + N_EXPERTS * bm
    n_blocks = l_rows // bm
    tok_buf = jnp.zeros((l_rows,), jnp.int32).at[dest].set(jnp.arange(nk, dtype=jnp.int32) // TOP_K)
    block_expert = jnp.clip(
        jnp.searchsorted(pad_end, jnp.arange(n_blocks, dtype=jnp.int32) * bm, side="right"), 0, N_EXPERTS - 1
    ).astype(jnp.int32)
    n_used = (pad_end[-1] // bm).astype(jnp.int32).reshape(1)

    y_buf = moe_ffn_grouped(block_expert, n_used, tok_buf, x, g, wg, wu, wd, bm=bm, tf=tf, out_dtype=y_dtype)
    dest2 = dest.reshape(m, TOP_K)
    y0 = jnp.take(y_buf, dest2[:, 0], axis=0)
    y1 = jnp.take(y_buf, dest2[:, 1], axis=0)
    return combine_norm(x, y0, y1, gates, g_final, tm=tm)


def _sample_mix_weights(sgu_w, sgu_b, t_new, n_seq):
    r = jnp.arange(n_seq * t_new)
    t = r % t_new
    same = (r[:, None] // t_new) == (r[None, :] // t_new)
    causal = t[None, :] <= t[:, None]
    w = sgu_w[:, t[:, None], t[None, :]] * (same & causal)[None].astype(sgu_w.dtype)
    bias = jnp.repeat(sgu_b[:, t].T, CHUNK, axis=1)
    return w, bias


def kernel(x_prompt, x_sample, mem_prompt, cache_attn_k, cache_attn_v, cache_mem_k, cache_mem_v, page_table, norm_mix, w_in, b_gate, sgu_ln_g, sgu_ln_b, sgu_w, sgu_b, lam_q1, lam_k1, lam_q2, lam_k2, subln_g, mem_norm, w_mem_k, w_mem_v, w_br_a, w_br_b, w_br_m, w_out, norm_ffn, w_d_gate, w_d_up, w_d_down, w_router, b_router, w_e_gate, w_e_up, w_e_down, norm_final):
    batch, seq, d = x_prompt.shape
    n_seq, t_new, _ = x_sample.shape
    depth = w_in.shape[0]
    n_pages = page_table.shape[1]
    past_len = n_pages * PAGE_SIZE
    mp = batch * seq
    ms = n_seq * t_new
    assert ms == CHUNK and d == D_MODEL and N_QROWS == 2 * DA_HEADS * t_new

    xp = x_prompt.reshape(mp, d)
    xs = x_sample.reshape(ms, d)
    mem = mem_prompt.reshape(batch * N_MEM, d)
    slopes =jnp.asarray([2.0 ** (-8.0 * (i + 1) / DA_HEADS) for i in range(DA_HEADS)], F32)
    row_slopes = jnp.repeat(slopes, 2 * t_new).reshape(N_QROWS, 1)
    tril = jnp.tril(jnp.ones((CHUNK, CHUNK), F32))

    c_q = 2 * SGU_WIDTH
    c_k = c_q + DA_WIDTH
    c_qm = c_k + 2 * DA_WIDTH
    c_gate = c_qm + MEM_WIDTH

    outs = {k: [] for k in ("pk", "pv", "pmk", "pmv", "sk", "sv", "scv")}
    yp = ys = None
    for l in range(depth):
        lam_init = 0.8 - 0.6 * math.exp(-0.3 * l)
        lamp = jnp.stack([lam_q1[l], lam_k1[l], lam_q2[l], lam_k2[l]])
        w_l = w_in[l]
        w_z = jnp.concatenate([w_l[:, :c_k], w_l[:, c_qm:c_gate]], axis=1).astype(BF16)
        w_kv = w_l[:, c_k:c_qm].astype(BF16)
        w_g = w_l[:, c_gate:].astype(BF16)
        zero_bias = jnp.zeros((w_z.shape[1],), F32)
        w_mem = jnp.concatenate([w_mem_k[l], w_mem_v[l]], axis=1).astype(BF16)
        wa, wb, wm, wo = (w.astype(BF16) for w in (w_br_a[l], w_br_b[l], w_br_m[l], w_out[l]))
        w_mix_p = (sgu_w[l] * tril[None]).astype(BF16)
        bias_p = jnp.repeat(sgu_b[l].T, CHUNK, axis=1)
        w_mix_s, bias_s = _sample_mix_weights(sgu_w[l], sgu_b[l], t_new, n_seq)
        qm_blk = c_k // MEM_WIDTH

        mkv = norm_matmul(mem, mem_norm[l], w_mem, jnp.zeros((2 * MEM_WIDTH,), F32), sigmoid=False,
                          out_dtype=F32, tm=batch * N_MEM, tn=MEM_WIDTH)
        mk_p = mkv[:, :MEM_WIDTH].reshape(batch, N_MEM, MEM_WIDTH)
        mv_p = mkv[:, MEM_WIDTH:].reshape(batch, N_MEM, MEM_WIDTH)

        z = norm_matmul(xp, norm_mix[l], w_z, zero_bias, sigmoid=False, out_dtype=BF16, tm=1024, tn=1280)
        kf, vf, kb, vt = norm_kv(xp, norm_mix[l], w_kv, tm=512, batch=batch)
        gates = norm_matmul(xp, norm_mix[l], w_g, b_gate[l], sigmoid=True, out_dtype=BF16, tm=1024, tn=1024)
        (a,) = sgu(z, sgu_ln_g[l], sgu_ln_b[l], w_mix_p, bias_p, n_chunks=8, emit_v=False)
        bo = diff_attn_prompt(z, kb, vt, slopes, lamp, subln_g[l], batch=batch, seq=seq, lam_init=lam_init,
                              q_col0=c_q // DA_V_DIM, tq=1024)
        mo = mem_attn(z, mk_p, mv_p, n_batch=batch, rows_per_batch=seq, tq=1024, q_col0=qm_blk)
        xp = merge(xp, a, bo, mo, gates, wa, wb, wm, wo, tm=512)

        zs = norm_matmul(xs, norm_mix[l], w_z, zero_bias, sigmoid=False, out_dtype=F32, tm=ms, tn=512)
        kfs, vfs, _, _ = norm_kv(xs, norm_mix[l], w_kv, tm=ms, batch=1)
        gates_s = norm_matmul(xs, norm_mix[l], w_g, b_gate[l], sigmoid=True, out_dtype=F32, tm=ms, tn=512)
        a_s, vrows = sgu(zs, sgu_ln_g[l], sgu_ln_b[l], w_mix_s.astype(BF16), bias_s, n_chunks=1, emit_v=True)
        q_s = zs[:, c_q:c_k].reshape(n_seq, 1, t_new, DA_WIDTH)
        q_rep = jnp.broadcast_to(q_s, (n_seq, 2 * DA_HEADS, t_new, DA_WIDTH)).reshape(n_seq, N_QROWS, DA_WIDTH)
        bo_s = diff_attn_sample(page_table, q_rep, kfs.reshape(n_seq, t_new, DA_WIDTH),
                                vfs.reshape(n_seq, t_new, DA_WIDTH), cache_attn_k, cache_attn_v, lamp, row_slopes,
                                subln_g[l], layer=l, past_len=past_len, lam_init=lam_init, pages_per_step=16)
        qm_s = zs[:, c_k:c_k + MEM_WIDTH].reshape(n_seq, t_new, MEM_WIDTH)
        qm_s = jnp.pad(qm_s, ((0, 0), (0, QM_PAD_ROWS - t_new), (0, 0))).reshape(n_seq * QM_PAD_ROWS, MEM_WIDTH)
        mo_s = mem_attn(qm_s, cache_mem_k[l].reshape(n_seq, N_MEM, MEM_WIDTH),
                        cache_mem_v[l].reshape(n_seq, N_MEM, MEM_WIDTH), n_batch=n_seq,
                        rows_per_batch=QM_PAD_ROWS, tq=QM_PAD_ROWS, q_col0=0)
        mo_s = mo_s.reshape(n_seq, QM_PAD_ROWS, MEM_WIDTH)[:, :t_new].reshape(ms, MEM_WIDTH)
        xs = merge(xs, a_s, bo_s.reshape(ms, DA_WIDTH), mo_s, gates_s, wa, wb, wm, wo, tm=ms)

        if l % 2 == 0:
            i = l // 2
            wg, wu, wd = w_d_gate[i].astype(BF16), w_d_up[i].astype(BF16), w_d_down[i].astype(BF16)
            xp = ffn_dense(xp, norm_ffn[l], wg, wu, wd, tm=512, tf=1408)
            xs = ffn_dense(xs, norm_ffn[l], wg, wu, wd, tm=ms, tf=1408)
        else:
            i = l // 2
            wg, wu, wd = w_e_gate[i].astype(BF16), w_e_up[i].astype(BF16), w_e_down[i].astype(BF16)
            xp, yp = moe_layer(xp, norm_ffn[l], w_router[i], b_router[i], wg, wu, wd, norm_final, tm=512, bm=512,
                               tf=1792, y_dtype=BF16)
            xs, ys = moe_layer(xs, norm_ffn[l], w_router[i], b_router[i], wg, wu, wd, norm_final, tm=ms, bm=128,
                               tf=1792, y_dtype=F32)

        outs["pk"].append(kf.reshape(batch, seq, DA_HEADS, DA_V_DIM))
        outs["pv"].append(vf.reshape(batch, seq, DA_HEADS, DA_V_DIM))
        outs["pmk"].append(mk_p.reshape(batch, N_MEM, MEM_HEADS, MEM_HEAD_DIM))
        outs["pmv"].append(mv_p.reshape(batch, N_MEM, MEM_HEADS, MEM_HEAD_DIM))
        outs["sk"].append(kfs.reshape(n_seq, t_new, DA_HEADS, DA_V_DIM))
        outs["sv"].append(vfs.reshape(n_seq, t_new, DA_HEADS, DA_V_DIM))
        outs["scv"].append(vrows.reshape(n_seq, t_new, SGU_WIDTH))

    if depth % 2 == 1:
        raise NotImplementedError("final norm is fused into the expert layer, which must come last")
    return (yp.reshape(batch, seq, d), ys.reshape(n_seq, t_new, d),
            jnp.stack(outs["pk"]), jnp.stack(outs["pv"]), jnp.stack(outs["pmk"]), jnp.stack(outs["pmv"]),
            jnp.stack(outs["sk"]), jnp.stack(outs["sv"]), jnp.stack(outs["scv"]))
```
